```python
import jax, jax.numpy as jnp
from jax import lax
import numpy as np

D_MODEL = 1024
BATCH = 4
SEQ = 4096
DEPTH = 4
DEC_BATCH = 128
DEC_SEQ = 1
PAST_LEN = 2048
PAGE_SIZE = 128

D_PLE = 256
N_AB_LAYERS = (DEPTH + 1) // 2
N_C_LAYERS = DEPTH // 2
H_A = 8
HD_A = 64
D_A = H_A * HD_A
D_B = D_MODEL - D_A
CONV_B = 3
D_RNN = D_MODEL
N_BLK_C = 8
BS_C = D_RNN // N_BLK_C
CONV_C = 4
LRU_C = 8.0
D_FF = 2816
Q_BLOCK = 128
N_NORMS = 7
EPS = 1e-6
NEG = -1e30
D_IN_AB = 3 * D_A + H_A + 3 * D_B
AB_SPLITS = (D_A, 2 * D_A, 3 * D_A, 3 * D_A + H_A, 3 * D_A + H_A + D_B, 3 * D_A + H_A + 2 * D_B)

kernel_name = 'fox_conv_rglru_hybrid_step'


def rmsnorm(x, g):
    xf = x.astype(jnp.float32)
    y = xf * lax.rsqrt(jnp.mean(xf * xf, axis=-1, keepdims=True) + EPS)
    return (y * g.astype(jnp.float32)).astype(x.dtype)


def swiglu(x, wi, wo):
    g, u = jnp.split(x @ wi, 2, axis=-1)
    return (jax.nn.silu(g) * u) @ wo


def ffn_half(x, g_pre, g_post, wi, wo):
    return x + 0.5 * rmsnorm(swiglu(rmsnorm(x, g_pre), wi, wo), g_post)


def ple_add(x, p, g, w_pe, w_pg):
    gate = jax.nn.sigmoid(rmsnorm(x, g) @ w_pg)
    return x + (p @ w_pe) * gate


def causal_dwconv(u, buf, w):
    width = w.shape[0]
    t = u.shape[1]
    full = jnp.concatenate([buf.astype(u.dtype), u], axis=1)
    y = full[:, 0:t] * w[0]
    for j in range(1, width):
        y = y + full[:, j:j + t] * w[j]
    return y, full[:, -(width - 1):]


def fox_prompt(q, k, v, logf):
    bsz, s_len, nh, hd = q.shape
    nb = s_len // Q_BLOCK
    ch = jnp.cumsum(logf, axis=1).transpose(0, 2, 1)
    qb = q.reshape(bsz, nb, Q_BLOCK, nh, hd).transpose(1, 0, 2, 3, 4)
    cqb = ch.reshape(bsz, nh, nb, Q_BLOCK).transpose(2, 0, 1, 3)
    kpos = jnp.arange(s_len)
    scale = hd ** -0.5

    def block(args):
        i, qi, ci = args
        s = jnp.einsum('bqhd,bkhd->bhqk', qi, k).astype(jnp.float32) * scale
        s = s + ci[..., :, None] - ch[:, :, None, :]
        qpos = i * Q_BLOCK + jnp.arange(Q_BLOCK)
        s = jnp.where(kpos[None, :] <= qpos[:, None], s, NEG)
        p = jax.nn.softmax(s, axis=-1).astype(v.dtype)
        return jnp.einsum('bhqk,bkhd->bqhd', p, v)

    o = lax.map(block, (jnp.arange(nb), qb, cqb))
    return o.transpose(1, 0, 2, 3, 4).reshape(bsz, s_len, nh * hd)


def fox_decode(q, k_new, v_new, logf_new, k_past, v_past, logf_past):
    bsz, t, nh, hd = q.shape
    p_len = k_past.shape[1]
    k = jnp.concatenate([k_past.astype(k_new.dtype), k_new], axis=1)
    v = jnp.concatenate([v_past.astype(v_new.dtype), v_new], axis=1)
    logf = jnp.concatenate([logf_past.astype(jnp.float32), logf_new.astype(jnp.float32)], axis=1)
    ch = jnp.cumsum(logf, axis=1).transpose(0, 2, 1)
    s = jnp.einsum('bqhd,bkhd->bhqk', q, k).astype(jnp.float32) * (hd ** -0.5)
    s = s + ch[:, :, p_len:, None] - ch[:, :, None, :]
    kpos = jnp.arange(p_len + t)
    qpos = p_len + jnp.arange(t)
    s = jnp.where(kpos[None, :] <= qpos[:, None], s, NEG)
    p = jax.nn.softmax(s, axis=-1).astype(v.dtype)
    return jnp.einsum('bhqk,bkhd->bqhd', p, v).reshape(bsz, t, nh * hd)


def ab_mixer(h, conv_buf, past, w_in, b_f, conv_w, w_out):
    bsz, t, _ = h.shape
    q, k, v, f_lin, b_gate, c_gate, x_in = jnp.split(h @ w_in, AB_SPLITS, axis=-1)
    q = q.reshape(bsz, t, H_A, HD_A)
    k = k.reshape(bsz, t, H_A, HD_A)
    v = v.reshape(bsz, t, H_A, HD_A)
    logf = jax.nn.log_sigmoid(f_lin.astype(jnp.float32) + b_f.astype(jnp.float32))
    if past is None:
        y_a = fox_prompt(q, k, v, logf)
    else:
        y_a = fox_decode(q, k, v, logf, past[0], past[1], past[2])
    conv, new_buf = causal_dwconv(c_gate * x_in, conv_buf, conv_w)
    y_b = b_gate * conv
    y = jnp.concatenate([y_a, y_b.astype(y_a.dtype)], axis=-1) @ w_out
    return y, k, v, logf.astype(h.dtype), new_buf


def rglru(xc, h0, w_rg, b_rg, w_ig, b_ig, lam):
    bsz, t, _ = xc.shape
    xb = xc.reshape(bsz, t, N_BLK_C, BS_C)
    r = jax.nn.sigmoid(jnp.einsum('btnc,ncd->btnd', xb, w_rg).reshape(bsz, t, D_RNN) + b_rg)
    i = jax.nn.sigmoid(jnp.einsum('btnc,ncd->btnd', xb, w_ig).reshape(bsz, t, D_RNN) + b_ig)
    log_a = -LRU_C * r.astype(jnp.float32) * jax.nn.softplus(-lam.astype(jnp.float32))
    a = jnp.exp(log_a)
    u = jnp.sqrt(-jnp.expm1(2.0 * log_a)) * (i * xc).astype(jnp.float32)

    def step(hc, au):
        hc = au[0] * hc + au[1]
        return hc, hc

    h_last, hs = lax.scan(step, h0.astype(jnp.float32), (a.transpose(1, 0, 2), u.transpose(1, 0, 2)))
    return hs.transpose(1, 0, 2).astype(xc.dtype), h_last.astype(xc.dtype)


def c_mixer(h, conv_buf, h0, w_in, conv_w, conv_b, w_rg, b_rg, w_ig, b_ig, lam, w_out):
    gate, xr = jnp.split(h @ w_in, 2, axis=-1)
    xc, new_buf = causal_dwconv(xr, conv_buf, conv_w)
    xc = xc + conv_b
    hs, h_last = rglru(xc, h0, w_rg, b_rg, w_ig, b_ig, lam)
    y = (jax.nn.gelu(gate) * hs) @ w_out
    return y, new_buf, h_last


def setup_inputs(seed: int = 0) -> dict:
    key = jax.random.key(seed)
    ks = jax.random.split(key, 40)
    f32 = jnp.float32
    n_pages = PAST_LEN // PAGE_SIZE
    n_used = DEC_BATCH * n_pages
    n_phys = n_used + max(1, n_used // 4)
    nrm = lambda k, shape, s: jax.random.normal(k, shape, f32) * s
    page_table = jax.random.permutation(ks[0], n_phys)[:n_used].reshape(DEC_BATCH, n_pages).astype(jnp.int32)
    a0 = jax.random.uniform(ks[1], (N_C_LAYERS, D_RNN), f32, 0.9, 0.999)
    s0 = a0 ** (1.0 / LRU_C)
    lam = jnp.log(s0) - jnp.log1p(-s0)
    return {
        'x_prompt': nrm(ks[2], (BATCH, SEQ, D_MODEL), 1.0),
        'x_sample': nrm(ks[3], (DEC_BATCH, DEC_SEQ, D_MODEL), 1.0),
        'p_prompt': nrm(ks[4], (DEPTH, BATCH, SEQ, D_PLE), 1.0),
        'p_sample': nrm(ks[5], (DEPTH, DEC_BATCH, DEC_SEQ, D_PLE), 1.0),
        'cache_k': nrm(ks[6], (n_phys, N_AB_LAYERS, PAGE_SIZE, H_A, HD_A), 1.0),
        'cache_v': nrm(ks[7], (n_phys, N_AB_LAYERS, PAGE_SIZE, H_A, HD_A), 1.0),
        'cache_logf': jax.nn.log_sigmoid(1.0 + nrm(ks[8], (n_phys, N_AB_LAYERS, PAGE_SIZE, H_A), 1.0)),
        'state_conv_b': nrm(ks[9], (DEC_BATCH, N_AB_LAYERS, CONV_B - 1, D_B), 1.0),
        'state_conv_c': nrm(ks[10], (DEC_BATCH, N_C_LAYERS, CONV_C - 1, D_RNN), 1.0),
        'state_h_c': nrm(ks[11], (DEC_BATCH, N_C_LAYERS, D_RNN), 0.5),
        'page_table': page_table,
        'norms': 1.0 + nrm(ks[12], (DEPTH, N_NORMS, D_MODEL), 0.05),
        'ffn1_wi': nrm(ks[13], (DEPTH, D_MODEL, 2 * D_FF), D_MODEL ** -0.5),
        'ffn1_wo': nrm(ks[14], (DEPTH, D_FF, D_MODEL), D_FF ** -0.5),
        'ffn2_wi': nrm(ks[15], (DEPTH, D_MODEL, 2 * D_FF), D_MODEL ** -0.5),
        'ffn2_wo': nrm(ks[16], (DEPTH, D_FF, D_MODEL), D_FF ** -0.5),
        'ple_w_pe': nrm(ks[17], (DEPTH, D_PLE, D_MODEL), D_PLE ** -0.5),
        'ple_w_pg': nrm(ks[18], (DEPTH, D_MODEL, D_MODEL), D_MODEL ** -0.5),
        'ab_w_in': nrm(ks[19], (N_AB_LAYERS, D_MODEL, D_IN_AB), D_MODEL ** -0.5),
        'ab_b_f': 1.0 + nrm(ks[20], (N_AB_LAYERS, H_A), 0.1),
        'ab_conv_w': nrm(ks[21], (N_AB_LAYERS, CONV_B, D_B), CONV_B ** -0.5),
        'ab_w_out': nrm(ks[22], (N_AB_LAYERS, D_A + D_B, D_MODEL), (D_A + D_B) ** -0.5),
        'c_w_in': nrm(ks[23], (N_C_LAYERS, D_MODEL, 2 * D_RNN), D_MODEL ** -0.5),
        'c_conv_w': nrm(ks[24], (N_C_LAYERS, CONV_C, D_RNN), CONV_C ** -0.5),
        'c_conv_b': nrm(ks[25], (N_C_LAYERS, D_RNN), 0.01),
        'c_w_rg': nrm(ks[26], (N_C_LAYERS, N_BLK_C, BS_C, BS_C), BS_C ** -0.5),
        'c_b_rg': nrm(ks[27], (N_C_LAYERS, D_RNN), 0.01),
        'c_w_ig': nrm(ks[28], (N_C_LAYERS, N_BLK_C, BS_C, BS_C), BS_C ** -0.5),
        'c_b_ig': nrm(ks[29], (N_C_LAYERS, D_RNN), 0.01),
        'c_lam': lam,
        'c_w_out': nrm(ks[30], (N_C_LAYERS, D_RNN, D_MODEL), D_RNN ** -0.5),
    }


def reference(x_prompt, x_sample, p_prompt, p_sample, cache_k, cache_v, cache_logf, state_conv_b,
              state_conv_c, state_h_c, page_table, norms, ffn1_wi, ffn1_wo, ffn2_wi, ffn2_wo,
              ple_w_pe, ple_w_pg, ab_w_in, ab_b_f, ab_conv_w, ab_w_out, c_w_in, c_conv_w, c_conv_b,
              c_w_rg, c_b_rg, c_w_ig, c_b_ig, c_lam, c_w_out):
    xp, xs = x_prompt, x_sample
    bp = xp.shape[0]
    bs = xs.shape[0]
    kp_l, vp_l, fp_l, ks_l, vs_l, fs_l = [], [], [], [], [], []
    cbp_l, cbs_l, ccp_l, ccs_l, hp_l, hs_l = [], [], [], [], [], []
    for i in range(DEPTH):
        g = norms[i]
        xp = ffn_half(xp, g[0], g[1], ffn1_wi[i], ffn1_wo[i])
        xs = ffn_half(xs, g[0], g[1], ffn1_wi[i], ffn1_wo[i])
        hp = rmsnorm(xp, g[2])
        hs = rmsnorm(xs, g[2])
        l = i // 2
        if i % 2 == 0:
            buf0 = jnp.zeros((bp, CONV_B - 1, D_B), hp.dtype)
            yp, kp, vp, fp, cbp = ab_mixer(hp, buf0, None, ab_w_in[l], ab_b_f[l], ab_conv_w[l], ab_w_out[l])
            k_past = cache_k[page_table, l].reshape(bs, -1, H_A, HD_A)
            v_past = cache_v[page_table, l].reshape(bs, -1, H_A, HD_A)
            f_past = cache_logf[page_table, l].reshape(bs, -1, H_A)
            ys, ks_, vs_, fs_, cbs = ab_mixer(hs, state_conv_b[:, l], (k_past, v_past, f_past),
                                              ab_w_in[l], ab_b_f[l], ab_conv_w[l], ab_w_out[l])
            kp_l.append(kp); vp_l.append(vp); fp_l.append(fp)
            ks_l.append(ks_); vs_l.append(vs_); fs_l.append(fs_)
            cbp_l.append(cbp); cbs_l.append(cbs)
        else:
            buf0 = jnp.zeros((bp, CONV_C - 1, D_RNN), hp.dtype)
            h0 = jnp.zeros((bp, D_RNN), hp.dtype)
            yp, ccp, hlp = c_mixer(hp, buf0, h0, c_w_in[l], c_conv_w[l], c_conv_b[l], c_w_rg[l], c_b_rg[l],
                                   c_w_ig[l], c_b_ig[l], c_lam[l], c_w_out[l])
            ys, ccs, hls = c_mixer(hs, state_conv_c[:, l], state_h_c[:, l], c_w_in[l], c_conv_w[l], c_conv_b[l],
                                   c_w_rg[l], c_b_rg[l], c_w_ig[l], c_b_ig[l], c_lam[l], c_w_out[l])
            ccp_l.append(ccp); ccs_l.append(ccs); hp_l.append(hlp); hs_l.append(hls)
        xp = xp + rmsnorm(yp, g[3])
        xs = xs + rmsnorm(ys, g[3])
        xp = ffn_half(xp, g[4], g[5], ffn2_wi[i], ffn2_wo[i])
        xs = ffn_half(xs, g[4], g[5], ffn2_wi[i], ffn2_wo[i])
        xp = ple_add(xp, p_prompt[i], g[6], ple_w_pe[i], ple_w_pg[i])
        xs = ple_add(xs, p_sample[i], g[6], ple_w_pe[i], ple_w_pg[i])
    st = lambda lst: jnp.stack(lst, axis=1)
    return (xp, xs, st(kp_l), st(vp_l), st(fp_l), st(ks_l), st(vs_l), st(fs_l),
            st(cbp_l), st(cbs_l), st(ccp_l), st(ccs_l), st(hp_l), st(hs_l))
```

```python
import functools

import jax
import jax.numpy as jnp
from jax import lax
from jax.experimental import pallas as pl
from jax.experimental.pallas import tpu as pltpu

F32 = jnp.float32
BF16 = jnp.bfloat16

D_MODEL = 1024
D_PLE = 256
H_A = 8
HD_A = 64
D_A = H_A * HD_A
D_B = D_MODEL - D_A
CONV_B = 3
D_RNN = D_MODEL
N_BLK_C = 8
BS_C = D_RNN // N_BLK_C
CONV_C = 4
LRU_C = 8.0
D_FF = 2816
EPS = 1e-6
NEG = -1e30
PAGE_SIZE = 128

LANES = 128
SUBLANES = 8
FF_CHUNK = 256
VMEM_LIMIT = 56 * 1024 * 1024


def _cparams(*sem):
    return pltpu.CompilerParams(dimension_semantics=sem, vmem_limit_bytes=VMEM_LIMIT)


def _rms(x, g):
    return x * lax.rsqrt(jnp.mean(x * x, axis=-1, keepdims=True) + EPS) * g


def _dot(a, b):
    return jnp.dot(a, b, preferred_element_type=F32)


def _log_sigmoid(x):
    return jnp.minimum(x, 0.0) - jnp.log1p(jnp.exp(-jnp.abs(x)))


def _softplus(x):
    return jnp.maximum(x, 0.0) + jnp.log1p(jnp.exp(-jnp.abs(x)))


def _sigmoid(x):
    return 1.0 / (1.0 + jnp.exp(-x))


def _gelu_tanh(x):
    c = 0.7978845608028654
    return 0.5 * x * (1.0 + jnp.tanh(c * (x + 0.044715 * (x * x * x))))


def _row_spec(tm, d):
    return pl.BlockSpec((tm, d), lambda i: (i, 0))


def _full_spec(shape):
    nd = len(shape)
    return pl.BlockSpec(shape, lambda i: (0,) * nd)


def _ffn_kernel(x_ref, gpre_ref, gpost_ref, wig_ref, wiu_ref, wo_ref, o_ref, xn_ref, acc_ref):
    j = pl.program_id(1)

    @pl.when(j == 0)
    def _():
        xn_ref[...] = _rms(x_ref[...], gpre_ref[...]).astype(BF16)
        acc_ref[...] = jnp.zeros_like(acc_ref)

    xn = xn_ref[...]
    g = _dot(xn, wig_ref[...].astype(BF16))
    u = _dot(xn, wiu_ref[...].astype(BF16))
    h = (g * _sigmoid(g)) * u
    acc_ref[...] += _dot(h.astype(BF16), wo_ref[...].astype(BF16))

    @pl.when(j == pl.num_programs(1) - 1)
    def _():
        o_ref[...] = x_ref[...] + 0.5 * _rms(acc_ref[...], gpost_ref[...])


def _ffn_half(x, g_pre, g_post, wi, wo, tm):
    m = x.shape[0]
    nf = D_FF // FF_CHUNK
    return pl.pallas_call(
        _ffn_kernel,
        grid=(m // tm, nf),
        in_specs=[
            pl.BlockSpec((tm, D_MODEL), lambda i, j: (i, 0)),
            pl.BlockSpec((1, D_MODEL), lambda i, j: (0, 0)),
            pl.BlockSpec((1, D_MODEL), lambda i, j: (0, 0)),
            pl.BlockSpec((D_MODEL, FF_CHUNK), lambda i, j: (0, j)),
            pl.BlockSpec((D_MODEL, FF_CHUNK), lambda i, j: (0, j + D_FF // FF_CHUNK)),
            pl.BlockSpec((FF_CHUNK, D_MODEL), lambda i, j: (j, 0)),
        ],
        out_specs=pl.BlockSpec((tm, D_MODEL), lambda i, j: (i, 0)),
        out_shape=jax.ShapeDtypeStruct((m, D_MODEL), F32),
        scratch_shapes=[pltpu.VMEM((tm, D_MODEL), BF16), pltpu.VMEM((tm, D_MODEL), F32)],
        compiler_params=_cparams("parallel", "arbitrary"),
        name="ffn_half",
    )(x, g_pre, g_post, wi, wi, wo)


def _ple_kernel(x_ref, p_ref, g_ref, wpe_ref, wpg_ref, o_ref):
    x = x_ref[...]
    gate = _sigmoid(_dot(_rms(x, g_ref[...]).astype(BF16), wpg_ref[...]))
    o_ref[...] = x + _dot(p_ref[...].astype(BF16), wpe_ref[...]) * gate


def _ple_add(x, p, g, w_pe, w_pg, tm):
    m = x.shape[0]
    return pl.pallas_call(
        _ple_kernel,
        grid=(m // tm,),
        in_specs=[_row_spec(tm, D_MODEL), _row_spec(tm, D_PLE), _full_spec((1, D_MODEL)),
                  _full_spec((D_PLE, D_MODEL)), _full_spec((D_MODEL, D_MODEL))],
        out_specs=_row_spec(tm, D_MODEL),
        out_shape=jax.ShapeDtypeStruct((m, D_MODEL), F32),
        compiler_params=_cparams("parallel"),
        name="ple_add",
    )(x, p, g, w_pe, w_pg)


def _ab_proj_kernel(x_ref, g_ref, w_ref, bf_ref, q_ref, k_ref, v_ref, kb_ref, vb_ref, lf_ref, bg_ref, cx_ref):
    xn = _rms(x_ref[...], g_ref[...]).astype(BF16)
    col = lambda c: _dot(xn, w_ref[:, c * D_A:(c + 1) * D_A])
    q_ref[...] = (col(0) * (HD_A ** -0.5)).astype(BF16)
    k = col(1)
    k_ref[...] = k
    kb_ref[...] = k.astype(BF16)
    v = col(2)
    v_ref[...] = v
    vb_ref[...] = v.astype(BF16)
    bg_ref[...] = col(3)
    cx_ref[...] = col(4) * col(5)
    f_lin = _dot(xn, w_ref[:, 6 * D_A:6 * D_A + LANES])
    lf_ref[...] = _log_sigmoid(f_lin + bf_ref[...])


def _ab_proj(x, g, w, b_f, tm):
    m = x.shape[0]
    wide = lambda dt: jax.ShapeDtypeStruct((m, D_A), dt)
    return pl.pallas_call(
        _ab_proj_kernel,
        grid=(m // tm,),
        in_specs=[_row_spec(tm, D_MODEL), _full_spec((1, D_MODEL)), _full_spec((D_MODEL, 6 * D_A + LANES)),
                  _full_spec((1, LANES))],
        out_specs=[_row_spec(tm, D_A)] * 5 + [_row_spec(tm, LANES)] + [_row_spec(tm, D_A)] * 2,
        out_shape=[wide(BF16), wide(F32), wide(F32), wide(BF16), wide(BF16),
                   jax.ShapeDtypeStruct((m, LANES), F32), wide(F32), wide(F32)],
        compiler_params=_cparams("parallel"),
        name="ab_proj",
    )(x, g, w, b_f)


def _cumsum_kernel(lf_ref, c_ref, carry_ref):
    @pl.when(pl.program_id(1) == 0)
    def _():
        carry_ref[...] = jnp.zeros_like(carry_ref)

    x = lf_ref[...]
    n = x.shape[0]
    row = lax.broadcasted_iota(jnp.int32, (n, 1), 0)
    d = 1
    while d < n:
        x = x + jnp.where(row >= d, pltpu.roll(x, d, 0), 0.0)
        d *= 2
    x = x + carry_ref[...]
    c_ref[...] = x
    carry_ref[...] = x[n - 1:n, :]


def _cumsum_rows(lf, n_seq, s_len, tc):
    nb = s_len // tc
    return pl.pallas_call(
        _cumsum_kernel,
        grid=(n_seq, nb),
        in_specs=[pl.BlockSpec((tc, LANES), lambda b, j: (b * nb + j, 0))],
        out_specs=pl.BlockSpec((tc, LANES), lambda b, j: (b * nb + j, 0)),
        out_shape=jax.ShapeDtypeStruct(lf.shape, F32),
        scratch_shapes=[pltpu.VMEM((1, LANES), F32)],
        compiler_params=_cparams("parallel", "arbitrary"),
        name="logf_cumsum",
    )(lf)


def _fox_prompt_kernel(q_ref, k_ref, v_ref, ck_ref, o_ref, m_ref, l_ref, acc_ref):
    qi = pl.program_id(2)
    kj = pl.program_id(3)
    t = q_ref.shape[0]
    lane = lax.broadcasted_iota(jnp.int32, (1, LANES), 1)

    @pl.when(kj == 0)
    def _():
        m_ref[...] = jnp.full_like(m_ref, NEG)
        l_ref[...] = jnp.zeros_like(l_ref)
        acc_ref[...] = jnp.zeros_like(acc_ref)

    def step(diagonal):
        q = q_ref[...]
        k = k_ref[...]
        v = v_ref[...]
        for e in range(2):
            head = (lane < HD_A) if e == 0 else (lane >= HD_A)
            qe = jnp.where(head, q, jnp.zeros_like(q))
            s = lax.dot_general(qe, k, (((1,), (1,)), ((), ())), preferred_element_type=F32)
            s = s - ck_ref[e:e + 1, :]
            if diagonal:
                row = lax.broadcasted_iota(jnp.int32, (t, 1), 0)
                colk = lax.broadcasted_iota(jnp.int32, (1, t), 1)
                s = jnp.where(colk <= row, s, NEG)
            m_prev = m_ref[e]
            m_new = jnp.maximum(m_prev, jnp.max(s, axis=-1, keepdims=True))
            alpha = jnp.exp(m_prev - m_new)
            p = jnp.exp(s - m_new)
            l_ref[e] = alpha * l_ref[e] + jnp.sum(p, axis=-1, keepdims=True)
            acc_ref[e] = alpha * acc_ref[e] + _dot(p.astype(BF16), v)
            m_ref[e] = m_new

    @pl.when(kj < qi)
    def _():
        step(False)

    @pl.when(kj == qi)
    def _():
        step(True)
        o0 = acc_ref[0] / l_ref[0]
        o1 = acc_ref[1] / l_ref[1]
        o_ref[...] = jnp.where(lane < HD_A, o0, o1).astype(BF16)


def _fox_prompt(q, k, v, ck, n_seq, s_len, t):
    nb = s_len // t
    npair = D_A // LANES
    kv_spec = pl.BlockSpec((t, LANES), lambda b, h, i, j: (b * nb + jnp.minimum(i, j), h))
    return pl.pallas_call(
        _fox_prompt_kernel,
        grid=(n_seq, npair, nb, nb),
        in_specs=[
            pl.BlockSpec((t, LANES), lambda b, h, i, j: (b * nb + i, h)),
            kv_spec, kv_spec,
            pl.BlockSpec((None, None, 2, t), lambda b, h, i, j: (b, h, 0, jnp.minimum(i, j))),
        ],
        out_specs=pl.BlockSpec((t, LANES), lambda b, h, i, j: (b * nb + i, h)),
        out_shape=jax.ShapeDtypeStruct((n_seq * s_len, D_A), BF16),
        scratch_shapes=[pltpu.VMEM((2, t, 1), F32), pltpu.VMEM((2, t, 1), F32), pltpu.VMEM((2, t, LANES), F32)],
        compiler_params=_cparams("parallel", "parallel", "parallel", "arbitrary"),
        name="fox_prompt",
    )(q, k, v, ck)


def _fox_decode_kernel(pt_ref, q_ref, kn_ref, vn_ref, fn_ref, *refs, n_pages):
    del pt_ref
    k_refs = refs[:n_pages]
    v_refs = refs[n_pages:2 * n_pages]
    f_refs = refs[2 * n_pages:3 * n_pages]
    o_ref = refs[3 * n_pages]
    ps = PAGE_SIZE

    lane = lax.broadcasted_iota(jnp.int32, (H_A, D_A), 1)
    hrow = lax.broadcasted_iota(jnp.int32, (H_A, D_A), 0)
    own = (lane // HD_A) == hrow
    qbd_f = jnp.where(own, q_ref[...].astype(F32), 0.0)
    qbd = qbd_f.astype(BF16)

    pos = lax.broadcasted_iota(jnp.int32, (H_A, ps), 1)
    carry = fn_ref[...]
    bias = [None] * n_pages
    for j in reversed(range(n_pages)):
        lf = f_refs[j][...]
        x = lf
        d = 1
        while d < ps:
            x = x + jnp.where(pos + d < ps, pltpu.roll(x, ps - d, 1), 0.0)
            d *= 2
        bias[j] = (x - lf) + carry
        carry = carry + x[:, 0:1]

    kn = kn_ref[...].astype(BF16).astype(F32)
    s_new = jnp.sum(qbd_f * kn, axis=-1, keepdims=True)
    scores = []
    m = s_new
    for j in range(n_pages):
        s = lax.dot_general(qbd, k_refs[j][...].astype(BF16), (((1,), (1,)), ((), ())),
                            preferred_element_type=F32) + bias[j]
        scores.append(s)
        m = jnp.maximum(m, jnp.max(s, axis=-1, keepdims=True))
    p_new = jnp.exp(s_new - m)
    l = p_new
    vn = vn_ref[...].astype(BF16).astype(F32)
    acc = p_new.astype(BF16).astype(F32) * vn
    for j in range(n_pages):
        p = jnp.exp(scores[j] - m)
        l = l + jnp.sum(p, axis=-1, keepdims=True)
        acc = acc + _dot(p.astype(BF16), v_refs[j][...].astype(BF16))
    o = jnp.where(own, acc / l, 0.0)
    o_ref[...] = jnp.sum(o, axis=0, keepdims=True).astype(BF16)


def _fox_decode(page_table, q, k_new, v_new, f_new, cache_k, cache_v, cache_ft, layer):
    bsz, n_pages = page_table.shape
    row = lambda dt_shape: pl.BlockSpec((None,) + dt_shape, lambda b, pt: (b, 0, 0))
    kv_specs = [pl.BlockSpec((None, None, PAGE_SIZE, D_A), functools.partial(
        lambda b, pt, j: (pt[b, j], layer, 0, 0), j=j)) for j in range(n_pages)]
    f_specs = [pl.BlockSpec((None, None, H_A, PAGE_SIZE), functools.partial(
        lambda b, pt, j: (pt[b, j], layer, 0, 0), j=j)) for j in range(n_pages)]
    grid_spec = pltpu.PrefetchScalarGridSpec(
        num_scalar_prefetch=1,
        grid=(bsz,),
        in_specs=[row((1, D_A)), row((1, D_A)), row((1, D_A)), row((H_A, 1))] + kv_specs + kv_specs + f_specs,
        out_specs=row((1, D_A)),
    )
    return pl.pallas_call(
        functools.partial(_fox_decode_kernel, n_pages=n_pages),
        grid_spec=grid_spec,
        out_shape=jax.ShapeDtypeStruct((bsz, 1, D_A), BF16),
        compiler_params=_cparams("parallel"),
        name="fox_decode",
    )(page_table, q, k_new, v_new, f_new, *([cache_k] * n_pages), *([cache_v] * n_pages), *([cache_ft] * n_pages))


def _shifted_rows(cur, prev_ref, first, n_shift):
    tm = cur.shape[0]
    row = lax.broadcasted_iota(jnp.int32, (tm, 1), 0)
    keep = jnp.where(first, 0.0, 1.0)
    prev = prev_ref[...] * keep
    out = []
    for sft in range(1, n_shift + 1):
        r = pltpu.roll(cur, sft, 0)
        for i in range(sft):
            r = jnp.where(row == i, prev[SUBLANES - sft + i:SUBLANES - sft + i + 1, :], r)
        out.append(r)
    return out


def _ab_out_kernel(*refs, explicit_state, tiles_per_seq):
    if explicit_state:
        x_ref, g_ref, ya_ref, bg_ref, cx_ref, m1_ref, m2_ref, cw_ref, wo_ref, o_ref = refs
        cx = cx_ref[...]
        m1, m2 = m1_ref[...], m2_ref[...]
    else:
        x_ref, g_ref, ya_ref, bg_ref, cx_ref, prev_ref, cw_ref, wo_ref, o_ref = refs
        cx = cx_ref[...]
        first = (pl.program_id(0) % tiles_per_seq) == 0
        m1, m2 = _shifted_rows(cx, prev_ref, first, CONV_B - 1)
    conv = m2 * cw_ref[0:1, :] + m1 * cw_ref[1:2, :] + cx * cw_ref[2:3, :]
    yb = bg_ref[...] * conv
    y = _dot(ya_ref[...], wo_ref[0:D_A, :]) + _dot(yb.astype(BF16), wo_ref[D_A:, :])
    o_ref[...] = x_ref[...] + _rms(y, g_ref[...])


def _ab_out(x, g, ya, bg, cx, conv_w, w_out, tm, seq_len=None, state=None):
    m = x.shape[0]
    explicit = state is not None
    if explicit:
        extra = [state[1], state[0]]
        extra_specs = [_row_spec(tm, D_B)] * 2
        tiles_per_seq = 1
    else:
        extra = [cx]
        per = tm // SUBLANES
        extra_specs = [pl.BlockSpec((SUBLANES, D_B), lambda i: (jnp.maximum(i * per - 1, 0), 0))]
        tiles_per_seq = seq_len // tm
    return pl.pallas_call(
        functools.partial(_ab_out_kernel, explicit_state=explicit, tiles_per_seq=tiles_per_seq),
        grid=(m // tm,),
        in_specs=[_row_spec(tm, D_MODEL), _full_spec((1, D_MODEL)), _row_spec(tm, D_A), _row_spec(tm, D_B),
                  _row_spec(tm, D_B)] + extra_specs + [_full_spec((CONV_B, D_B)), _full_spec((D_MODEL, D_MODEL))],
        out_specs=_row_spec(tm, D_MODEL),
        out_shape=jax.ShapeDtypeStruct((m, D_MODEL), F32),
        compiler_params=_cparams("parallel"),
        name="ab_out",
    )(x, g, ya, bg, cx, *extra, conv_w, w_out)


def _c_proj_kernel(x_ref, g_ref, w_ref, gate_ref, xr_ref):
    xn = _rms(x_ref[...], g_ref[...]).astype(BF16)
    gate_ref[...] = _dot(xn, w_ref[:, 0:D_RNN])
    xr_ref[...] = _dot(xn, w_ref[:, D_RNN:])


def _c_proj(x, g, w, tm):
    m = x.shape[0]
    return pl.pallas_call(
        _c_proj_kernel,
        grid=(m // tm,),
        in_specs=[_row_spec(tm, D_MODEL), _full_spec((1, D_MODEL)), _full_spec((D_MODEL, 2 * D_RNN))],
        out_specs=[_row_spec(tm, D_RNN)] * 2,
        out_shape=[jax.ShapeDtypeStruct((m, D_RNN), F32)] * 2,
        compiler_params=_cparams("parallel"),
        name="c_proj",
    )(x, g, w)


def _lru_gates(xc, wgi_ref, brg_ref, big_ref, lam_ref):
    r_parts, i_parts = [], []
    for n in range(N_BLK_C):
        ri = _dot(xc[:, n * BS_C:(n + 1) * BS_C].astype(BF16), wgi_ref[n])
        r_parts.append(ri[:, :BS_C])
        i_parts.append(ri[:, BS_C:])
    r = _sigmoid(jnp.concatenate(r_parts, axis=-1) + brg_ref[...])
    ig = _sigmoid(jnp.concatenate(i_parts, axis=-1) + big_ref[...])
    log_a = -LRU_C * r * _softplus(-lam_ref[...])
    a = jnp.exp(log_a)
    th = jnp.tanh(log_a)
    u = jnp.sqrt(-2.0 * th / (1.0 - th)) * (ig * xc)
    return a, u


def _c_gates_prompt_kernel(xr_ref, prev_ref, cw_ref, cb_ref, wgi_ref, brg_ref, big_ref, lam_ref, a_ref, u_ref,
                           *, tiles_per_seq):
    xr = xr_ref[...]
    first = pl.program_id(1) == 0
    m1, m2, m3 = _shifted_rows(xr, prev_ref, first, CONV_C - 1)
    xc = m3 * cw_ref[0:1, :] + m2 * cw_ref[1:2, :] + m1 * cw_ref[2:3, :] + xr * cw_ref[3:4, :] + cb_ref[...]
    a, u = _lru_gates(xc, wgi_ref, brg_ref, big_ref, lam_ref)
    a_ref[...] = a
    u_ref[...] = u


def _c_gates_prompt(xr, conv_w, conv_b, wgi, b_rg, b_ig, lam, n_seq, s_len, tm):
    nt = s_len // tm
    per = tm // SUBLANES
    full = lambda shape: pl.BlockSpec(shape, lambda b, i: (0,) * len(shape))
    out_spec = pl.BlockSpec((tm, D_RNN), lambda b, i: (i, b))
    return pl.pallas_call(
        functools.partial(_c_gates_prompt_kernel, tiles_per_seq=nt),
        grid=(n_seq, nt),
        in_specs=[pl.BlockSpec((tm, D_RNN), lambda b, i: (b * nt + i, 0)),
                  pl.BlockSpec((SUBLANES, D_RNN), lambda b, i: (jnp.maximum((b * nt + i) * per - 1, 0), 0)),
                  full((CONV_C, D_RNN)), full((1, D_RNN)), full((N_BLK_C, BS_C, 2 * BS_C)),
                  full((1, D_RNN)), full((1, D_RNN)), full((1, D_RNN))],
        out_specs=[out_spec, out_spec],
        out_shape=[jax.ShapeDtypeStruct((s_len, n_seq * D_RNN), F32)] * 2,
        compiler_params=_cparams("parallel", "parallel"),
        name="c_gates_prompt",
    )(xr, xr, conv_w, conv_b, wgi, b_rg, b_ig, lam)


def _c_gates_sample_kernel(xr_ref, s0_ref, s1_ref, s2_ref, h0_ref, cw_ref, cb_ref, wgi_ref, brg_ref, big_ref,
                           lam_ref, h_ref):
    xr = xr_ref[...]
    xc = (s0_ref[...] * cw_ref[0:1, :] + s1_ref[...] * cw_ref[1:2, :] + s2_ref[...] * cw_ref[2:3, :]
          + xr * cw_ref[3:4, :] + cb_ref[...])
    a, u = _lru_gates(xc, wgi_ref, brg_ref, big_ref, lam_ref)
    h_ref[...] = a * h0_ref[...] + u


def _c_gates_sample(xr, state, h0, conv_w, conv_b, wgi, b_rg, b_ig, lam):
    m = xr.shape[0]
    return pl.pallas_call(
        _c_gates_sample_kernel,
        grid=(1,),
        in_specs=[_row_spec(m, D_RNN)] * 5 + [_full_spec((CONV_C, D_RNN)), _full_spec((1, D_RNN)),
                                             _full_spec((N_BLK_C, BS_C, 2 * BS_C)), _full_spec((1, D_RNN)),
                                             _full_spec((1, D_RNN)), _full_spec((1, D_RNN))],
        out_specs=_row_spec(m, D_RNN),
        out_shape=jax.ShapeDtypeStruct((m, D_RNN), F32),
        compiler_params=_cparams("arbitrary"),
        name="c_gates_sample",
    )(xr, state[0], state[1], state[2], h0, conv_w, conv_b, wgi, b_rg, b_ig, lam)


def _lru_scan_kernel(a_ref, u_ref, hs_ref, h_ref):
    @pl.when(pl.program_id(0) == 0)
    def _():
        h_ref[...] = jnp.zeros_like(h_ref)

    def body(t, h):
        h = a_ref[t] * h + u_ref[t]
        hs_ref[t] = h
        return h

    h_ref[...] = lax.fori_loop(0, a_ref.shape[0], body, h_ref[...], unroll=8)


def _lru_scan(a, u, ts):
    s_len, r, _ = a.shape
    spec = pl.BlockSpec((ts, r, LANES), lambda i: (i, 0, 0))
    return pl.pallas_call(
        _lru_scan_kernel,
        grid=(s_len // ts,),
        in_specs=[spec, spec],
        out_specs=spec,
        out_shape=jax.ShapeDtypeStruct(a.shape, F32),
        scratch_shapes=[pltpu.VMEM((r, LANES), F32)],
        compiler_params=_cparams("arbitrary"),
        name="lru_scan",
    )(a, u)


def _c_out_kernel(x_ref, g_ref, gate_ref, hs_ref, wo_ref, o_ref):
    z = _gelu_tanh(gate_ref[...]) * hs_ref[...]
    o_ref[...] = x_ref[...] + _rms(_dot(z.astype(BF16), wo_ref[...]), g_ref[...])


def _c_out(x, g, gate, hs, w_out, tm, time_major_seqs=None):
    m = x.shape[0]
    if time_major_seqs is None:
        hs_spec = _row_spec(tm, D_RNN)
    else:
        nt = m // time_major_seqs // tm
        hs_spec = pl.BlockSpec((tm, D_RNN), lambda i: (i % nt, i // nt))
    return pl.pallas_call(
        _c_out_kernel,
        grid=(m // tm,),
        in_specs=[_row_spec(tm, D_MODEL), _full_spec((1, D_MODEL)), _row_spec(tm, D_RNN), hs_spec,
                  _full_spec((D_RNN, D_MODEL))],
        out_specs=_row_spec(tm, D_MODEL),
        out_shape=jax.ShapeDtypeStruct((m, D_MODEL), F32),
        compiler_params=_cparams("parallel"),
        name="c_out",
    )(x, g, gate, hs, w_out)


def kernel(x_prompt, x_sample, p_prompt, p_sample, cache_k, cache_v, cache_logf, state_conv_b, state_conv_c, state_h_c, page_table, norms, ffn1_wi, ffn1_wo, ffn2_wi, ffn2_wo, ple_w_pe, ple_w_pg, ab_w_in, ab_b_f, ab_conv_w, ab_w_out, c_w_in, c_conv_w, c_conv_b, c_w_rg, c_b_rg, c_w_ig, c_b_ig, c_lam, c_w_out):
    bp, s_len, _ = x_prompt.shape
    bs = x_sample.shape[0]
    depth = norms.shape[0]
    n_phys, n_ab = cache_k.shape[0], cache_k.shape[1]
    mp = bp * s_len
    tm_p = 1024
    tm_x = 512
    t_attn = 512
    ts_scan = 256

    xp = x_prompt.reshape(mp, D_MODEL)
    xs = x_sample.reshape(bs, D_MODEL)
    pp = p_prompt.reshape(depth, mp, D_PLE)
    psm = p_sample.reshape(depth, bs, D_PLE)
    cache_k4 = cache_k.reshape(n_phys, n_ab, PAGE_SIZE, D_A)
    cache_v4 = cache_v.reshape(n_phys, n_ab, PAGE_SIZE, D_A)
    cache_ft = jnp.swapaxes(cache_logf, 2, 3)

    kp_l, vp_l, fp_l, ks_l, vs_l, fs_l = [], [], [], [], [], []
    cbp_l, cbs_l, ccp_l, ccs_l, hp_l, hs_l = [], [], [], [], [], []
    for i in range(depth):
        g = norms[i].reshape(norms.shape[1], 1, D_MODEL)
        xp = _ffn_half(xp, g[0], g[1], ffn1_wi[i], ffn1_wo[i], tm_p)
        xs = _ffn_half(xs, g[0], g[1], ffn1_wi[i], ffn1_wo[i], bs)
        l = i // 2
        if i % 2 == 0:
            w_in = ab_w_in[l]
            w_main = jnp.concatenate(
                [w_in[:, :3 * D_A], w_in[:, 3 * D_A + H_A:],
                 jnp.pad(w_in[:, 3 * D_A:3 * D_A + H_A], ((0, 0), (0, LANES - H_A)))], axis=1).astype(BF16)
            b_f = jnp.pad(ab_b_f[l], (0, LANES - H_A)).reshape(1, LANES)
            w_out = ab_w_out[l].astype(BF16)
            q, k, v, kb, vb, lf, bg, cx = _ab_proj(xp, g[2], w_main, b_f, tm_x)
            c = _cumsum_rows(lf, bp, s_len, 512)
            ck = c[:, :H_A].reshape(bp, s_len, D_A // LANES, 2).transpose(0, 2, 3, 1)
            ya = _fox_prompt(q, kb, vb, ck, bp, s_len, t_attn)
            xp = _ab_out(xp, g[3], ya, bg, cx, ab_conv_w[l], w_out, tm_x, seq_len=s_len)
            kp_l.append(k.reshape(bp, s_len, H_A, HD_A))
            vp_l.append(v.reshape(bp, s_len, H_A, HD_A))
            fp_l.append(lf[:, :H_A].reshape(bp, s_len, H_A))
            cbp_l.append(cx.reshape(bp, s_len, D_B)[:, s_len - (CONV_B - 1):])
            q, k, v, _, _, lf, bg, cx = _ab_proj(xs, g[2], w_main, b_f, bs)
            ya = _fox_decode(page_table, q.reshape(bs, 1, D_A), k.reshape(bs, 1, D_A), v.reshape(bs, 1, D_A),
                             lf[:, :H_A].reshape(bs, H_A, 1), cache_k4, cache_v4, cache_ft, l)
            st = state_conv_b[:, l]
            xs = _ab_out(xs, g[3], ya.reshape(bs, D_A), bg, cx, ab_conv_w[l], w_out, bs,
                         state=(st[:, 0], st[:, 1]))
            ks_l.append(k.reshape(bs, 1, H_A, HD_A))
            vs_l.append(v.reshape(bs, 1, H_A, HD_A))
            fs_l.append(lf[:, :H_A].reshape(bs, 1, H_A))
            cbs_l.append(jnp.stack([st[:, 1], cx], axis=1))
        else:
            w_in = c_w_in[l].astype(BF16)
            wgi = jnp.concatenate([c_w_rg[l], c_w_ig[l]], axis=-1).astype(BF16)
            row = lambda a: a.reshape(1, D_RNN)
            gate_args = (c_conv_w[l], row(c_conv_b[l]), wgi, row(c_b_rg[l]), row(c_b_ig[l]), row(c_lam[l]))
            w_out = c_w_out[l].astype(BF16)
            gate, xr = _c_proj(xp, g[2], w_in, tm_x)
            a, u = _c_gates_prompt(xr, *gate_args, bp, s_len, tm_x)
            rows = bp * D_RNN // LANES
            hs = _lru_scan(a.reshape(s_len, rows, LANES), u.reshape(s_len, rows, LANES), ts_scan)
            hs = hs.reshape(s_len, bp * D_RNN)
            xp = _c_out(xp, g[3], gate, hs, w_out, tm_x, time_major_seqs=bp)
            ccp_l.append(xr.reshape(bp, s_len, D_RNN)[:, s_len - (CONV_C - 1):])
            hp_l.append(hs[s_len - 1].reshape(bp, D_RNN))
            gate, xr = _c_proj(xs, g[2], w_in, bs)
            st = state_conv_c[:, l]
            h = _c_gates_sample(xr, (st[:, 0], st[:, 1], st[:, 2]), state_h_c[:, l], *gate_args)
            xs = _c_out(xs, g[3], gate, h, w_out, bs)
            ccs_l.append(jnp.stack([st[:, 1], st[:, 2], xr], axis=1))
            hs_l.append(h)
        xp = _ffn_half(xp, g[4], g[5], ffn2_wi[i], ffn2_wo[i], tm_p)
        xs = _ffn_half(xs, g[4], g[5], ffn2_wi[i], ffn2_wo[i], bs)
        w_pe = ple_w_pe[i].astype(BF16)
        w_pg = ple_w_pg[i].astype(BF16)
        xp = _ple_add(xp, pp[i], g[6], w_pe, w_pg, tm_x)
        xs = _ple_add(xs, psm[i], g[6], w_pe, w_pg, bs)
    st = lambda lst: jnp.stack(lst, axis=1)
    return (xp.reshape(bp, s_len, D_MODEL), xs.reshape(bs, 1, D_MODEL),
            st(kp_l), st(vp_l), st(fp_l), st(ks_l), st(vs_l), st(fs_l),
            st(cbp_l), st(cbs_l), st(ccp_l), st(ccs_l), st(hp_l), st(hs_l))
```

```python
import functools

import jax
import jax.numpy as jnp
from jax import lax
from jax.experimental import pallas as pl
from jax.experimental.pallas import tpu as pltpu

F32 = jnp.float32
BF16 = jnp.bfloat16

D_MODEL = 1024
D_PLE = 256
H_A = 8
HD_A = 64
D_A = H_A * HD_A
D_B = D_MODEL - D_A
CONV_B = 3
D_RNN = D_MODEL
N_BLK_C = 8
BS_C = D_RNN // N_BLK_C
CONV_C = 4
LRU_C = 8.0
D_FF = 2816
EPS = 1e-6
NEG = -1e30
PAGE_SIZE = 128

LANES = 128
SUBLANES = 8
BF16_ROWS = 16
FF_CHUNK = 256
N_PAIR = D_A // LANES
KVF_ROWS = 2 * D_A + BF16_ROWS
VMEM_LIMIT = 56 * 1024 * 1024

_NT = (((1,), (1,)), ((), ()))


def _cparams(*sem):
    return pltpu.CompilerParams(dimension_semantics=sem, vmem_limit_bytes=VMEM_LIMIT)


def _rms(x, g):
    return x * lax.rsqrt(jnp.mean(x * x, axis=-1, keepdims=True) + EPS) * g


def _dot(a, b):
    return jnp.dot(a, b, preferred_element_type=F32)


def _dot_nt(a, b):
    return lax.dot_general(a, b, _NT, preferred_element_type=F32)


def _log_sigmoid(x):
    return jnp.minimum(x, 0.0) - jnp.log1p(jnp.exp(-jnp.abs(x)))


def _softplus(x):
    return jnp.maximum(x, 0.0) + jnp.log1p(jnp.exp(-jnp.abs(x)))


def _sigmoid(x):
    return 1.0 / (1.0 + jnp.exp(-x))


def _gelu_tanh(x):
    c = 0.7978845608028654
    return 0.5 * x * (1.0 + jnp.tanh(c * (x + 0.044715 * (x * x * x))))


def _row_spec(tm, d):
    return pl.BlockSpec((tm, d), lambda i: (i, 0))


def _full_spec(shape):
    nd = len(shape)
    return pl.BlockSpec(shape, lambda i: (0,) * nd)


def _ffn_kernel(x_ref, gpre_ref, gpost_ref, wig_ref, wiu_ref, wo_ref, o_ref, xn_ref, acc_ref):
    j = pl.program_id(1)

    @pl.when(j == 0)
    def _():
        xn_ref[...] = _rms(x_ref[...], gpre_ref[...]).astype(BF16)
        acc_ref[...] = jnp.zeros_like(acc_ref)

    xn = xn_ref[...]
    g = _dot(xn, wig_ref[...].astype(BF16))
    u = _dot(xn, wiu_ref[...].astype(BF16))
    h = (g * _sigmoid(g)) * u
    acc_ref[...] += _dot(h.astype(BF16), wo_ref[...].astype(BF16))

    @pl.when(j == pl.num_programs(1) - 1)
    def _():
        o_ref[...] = x_ref[...] + 0.5 * _rms(acc_ref[...], gpost_ref[...])


def _ffn_half(x, g_pre, g_post, wi, wo, layer, tm):
    m = x.shape[0]
    nf = D_FF // FF_CHUNK
    return pl.pallas_call(
        _ffn_kernel,
        grid=(m // tm, nf),
        in_specs=[
            pl.BlockSpec((tm, D_MODEL), lambda i, j: (i, 0)),
            pl.BlockSpec((1, D_MODEL), lambda i, j: (0, 0)),
            pl.BlockSpec((1, D_MODEL), lambda i, j: (0, 0)),
            pl.BlockSpec((None, D_MODEL, FF_CHUNK), lambda i, j: (layer, 0, j)),
            pl.BlockSpec((None, D_MODEL, FF_CHUNK), lambda i, j: (layer, 0, j + D_FF // FF_CHUNK)),
            pl.BlockSpec((None, FF_CHUNK, D_MODEL), lambda i, j: (layer, j, 0)),
        ],
        out_specs=pl.BlockSpec((tm, D_MODEL), lambda i, j: (i, 0)),
        out_shape=jax.ShapeDtypeStruct((m, D_MODEL), F32),
        scratch_shapes=[pltpu.VMEM((tm, D_MODEL), BF16), pltpu.VMEM((tm, D_MODEL), F32)],
        compiler_params=_cparams("parallel", "arbitrary"),
        name="ffn_half",
    )(x, g_pre, g_post, wi, wi, wo)


def _ple_kernel(x_ref, p_ref, g_ref, wpe_ref, wpg_ref, o_ref):
    x = x_ref[...]
    gate = _sigmoid(_dot(_rms(x, g_ref[...]).astype(BF16), wpg_ref[...]))
    o_ref[...] = x + _dot(p_ref[...].astype(BF16), wpe_ref[...]) * gate


def _ple_add(x, p, g, w_pe, w_pg, layer, tm):
    m = x.shape[0]
    return pl.pallas_call(
        _ple_kernel,
        grid=(m // tm,),
        in_specs=[_row_spec(tm, D_MODEL), pl.BlockSpec((None, tm, D_PLE), lambda i: (layer, i, 0)),
                  _full_spec((1, D_MODEL)),
                  pl.BlockSpec((None, D_PLE, D_MODEL), lambda i: (layer, 0, 0)),
                  pl.BlockSpec((None, D_MODEL, D_MODEL), lambda i: (layer, 0, 0))],
        out_specs=_row_spec(tm, D_MODEL),
        out_shape=jax.ShapeDtypeStruct((m, D_MODEL), F32),
        compiler_params=_cparams("parallel"),
        name="ple_add",
    )(x, p, g, w_pe, w_pg)


def _ab_proj_kernel(x_ref, g_ref, wa_ref, wkvf_ref, bf_ref, q_ref, kt_ref, vt_ref, ktb_ref, vtb_ref, lft_ref,
                    bg_ref, cx_ref):
    tm = x_ref.shape[0]
    xn = _rms(x_ref[...], g_ref[...]).astype(BF16)
    col = lambda c: _dot(xn, wa_ref[:, c * D_A:(c + 1) * D_A])
    q_ref[...] = (col(0) * (HD_A ** -0.5)).astype(BF16)
    bg_ref[...] = col(1)
    cx_ref[...] = col(2) * col(3)
    kvf = _dot_nt(wkvf_ref[...], xn)
    kt = kvf[0:D_A]
    vt = kvf[D_A:2 * D_A]
    kt_ref[...] = kt
    vt_ref[...] = vt
    ktb_ref[...] = kt.astype(BF16).reshape(N_PAIR, LANES, tm)
    vtb_ref[...] = vt.astype(BF16).reshape(N_PAIR, LANES, tm)
    lft_ref[...] = _log_sigmoid(kvf[2 * D_A:2 * D_A + H_A] + bf_ref[...])


def _ab_proj(x, g, wa, wkvf, b_f, n_seq, tm):
    m = x.shape[0]
    s_len = m // n_seq
    nt = s_len // tm
    rows = lambda d: pl.BlockSpec((tm, d), lambda b, t: (b * nt + t, 0))
    full = lambda shape: pl.BlockSpec(shape, lambda b, t: (0,) * len(shape))
    t_spec = lambda r: pl.BlockSpec((None, r, tm), lambda b, t: (b, 0, t))
    tile_spec = pl.BlockSpec((None, N_PAIR, None, LANES, tm), lambda b, t: (b, 0, t, 0, 0))
    wide = lambda dt: jax.ShapeDtypeStruct((m, D_A), dt)
    t_shape = lambda r: jax.ShapeDtypeStruct((n_seq, r, s_len), F32)
    tile_shape = jax.ShapeDtypeStruct((n_seq, N_PAIR, nt, LANES, tm), BF16)
    return pl.pallas_call(
        _ab_proj_kernel,
        grid=(n_seq, nt),
        in_specs=[rows(D_MODEL), full((1, D_MODEL)), full((D_MODEL, 4 * D_A)), full((KVF_ROWS, D_MODEL)),
                  full((H_A, 1))],
        out_specs=[rows(D_A), t_spec(D_A), t_spec(D_A), tile_spec, tile_spec, t_spec(H_A), rows(D_B), rows(D_B)],
        out_shape=[wide(BF16), t_shape(D_A), t_shape(D_A), tile_shape, tile_shape, t_shape(H_A), wide(F32), wide(F32)],
        compiler_params=_cparams("parallel", "parallel"),
        name="ab_proj",
    )(x, g, wa, wkvf, b_f)


def _cumsum_kernel(x_ref, o_ref):
    x = x_ref[...]
    n = x.shape[1]
    pos = lax.broadcasted_iota(jnp.int32, (1, n), 1)
    d = 1
    while d < n:
        x = x + jnp.where(pos >= d, pltpu.roll(x, d, 1), 0.0)
        d *= 2
    o_ref[...] = x


def _cumsum_lanes(x):
    spec = pl.BlockSpec((None,) + x.shape[1:], lambda b: (b, 0, 0))
    return pl.pallas_call(
        _cumsum_kernel,
        grid=(x.shape[0],),
        in_specs=[spec],
        out_specs=spec,
        out_shape=jax.ShapeDtypeStruct(x.shape, F32),
        compiler_params=_cparams("parallel"),
        name="logf_cumsum",
    )(x)


def _fox_prompt_kernel(q_ref, kt_ref, vt_ref, ck_ref, o_ref):
    qi = pl.program_id(2)
    t = q_ref.shape[0]
    lane = lax.broadcasted_iota(jnp.int32, (1, LANES), 1)
    q = q_ref[...]
    zero = jnp.zeros_like(q)
    qs = (jnp.where(lane < HD_A, q, zero), jnp.where(lane >= HD_A, q, zero))

    def scores(j):
        kt = kt_ref[j]
        ck = ck_ref[j]
        return tuple(_dot(qs[e], kt) - ck[e:e + 1, :] for e in range(2))

    def accumulate(j, state, s_pair, diagonal):
        vt = vt_ref[j]
        out = []
        for e in range(2):
            m_prev, l_prev, acc_prev = state[e]
            s = s_pair[e]
            if diagonal:
                row = lax.broadcasted_iota(jnp.int32, (t, 1), 0)
                colk = lax.broadcasted_iota(jnp.int32, (1, t), 1)
                s = jnp.where(colk <= row, s, NEG)
            m_new = jnp.maximum(m_prev, jnp.max(s, axis=-1, keepdims=True))
            alpha = jnp.exp(m_prev - m_new)
            p = jnp.exp(s - m_new)
            l_new = alpha * l_prev + jnp.sum(p, axis=-1, keepdims=True)
            acc_new = alpha * acc_prev + _dot_nt(p.astype(BF16), vt)
            out.append((m_new, l_new, acc_new))
        return tuple(out)

    init = tuple((jnp.full((t, 1), NEG, F32), jnp.zeros((t, 1), F32), jnp.zeros((t, LANES), F32))
                 for _ in range(2))
    state = lax.fori_loop(0, qi, lambda j, st: accumulate(j, st, scores(j), False), init)
    (_, l0, a0), (_, l1, a1) = accumulate(qi, state, scores(qi), True)
    o_ref[...] = jnp.where(lane < HD_A, a0 / l0, a1 / l1).astype(BF16)


def _fox_prompt(q, ktb, vtb, ck):
    n_seq, _, nb, _, t = ktb.shape
    kv_spec = pl.BlockSpec((None, None, nb, LANES, t), lambda b, h, i: (b, h, 0, 0, 0))
    return pl.pallas_call(
        _fox_prompt_kernel,
        grid=(n_seq, N_PAIR, nb),
        in_specs=[pl.BlockSpec((t, LANES), lambda b, h, i: (b * nb + i, h)), kv_spec, kv_spec,
                  pl.BlockSpec((None, None, nb, 2, t), lambda b, h, i: (b, h, 0, 0, 0))],
        out_specs=pl.BlockSpec((t, LANES), lambda b, h, i: (b * nb + i, h)),
        out_shape=jax.ShapeDtypeStruct(q.shape, BF16),
        compiler_params=_cparams("parallel", "parallel", "parallel"),
        name="fox_prompt",
    )(q, ktb, vtb, ck)


def _fox_decode_kernel(pt_ref, q_ref, kn_ref, vn_ref, fn_ref, *refs, n_pages):
    del pt_ref
    k_refs = refs[:n_pages]
    v_refs = refs[n_pages:2 * n_pages]
    f_refs = refs[2 * n_pages:3 * n_pages]
    o_ref = refs[3 * n_pages]
    ps = PAGE_SIZE

    lane = lax.broadcasted_iota(jnp.int32, (H_A, D_A), 1)
    hrow = lax.broadcasted_iota(jnp.int32, (H_A, D_A), 0)
    own = (lane // HD_A) == hrow
    qbd_f = jnp.where(own, q_ref[...].astype(F32), 0.0)
    qbd = qbd_f.astype(BF16)

    pos = lax.broadcasted_iota(jnp.int32, (H_A, ps), 1)
    carry = fn_ref[...]
    bias = [None] * n_pages
    for j in reversed(range(n_pages)):
        lf = f_refs[j][...]
        x = lf
        d = 1
        while d < ps:
            x = x + jnp.where(pos + d < ps, pltpu.roll(x, ps - d, 1), 0.0)
            d *= 2
        bias[j] = (x - lf) + carry
        carry = carry + x[:, 0:1]

    kn = kn_ref[...].astype(BF16).astype(F32)
    s_new = jnp.sum(qbd_f * kn, axis=-1, keepdims=True)
    scores = []
    m = s_new
    for j in range(n_pages):
        s = _dot(qbd, k_refs[j][...].astype(BF16)) + bias[j]
        scores.append(s)
        m = jnp.maximum(m, jnp.max(s, axis=-1, keepdims=True))
    p_new = jnp.exp(s_new - m)
    l = p_new
    vn = vn_ref[...].astype(BF16).astype(F32)
    acc = p_new.astype(BF16).astype(F32) * vn
    for j in range(n_pages):
        p = jnp.exp(scores[j] - m)
        l = l + jnp.sum(p, axis=-1, keepdims=True)
        acc = acc + _dot_nt(p.astype(BF16), v_refs[j][...].astype(BF16))
    o = jnp.where(own, acc / l, 0.0)
    o_ref[...] = jnp.sum(o, axis=0, keepdims=True).astype(BF16)


def _fox_decode(page_table, q, k_new, v_new, f_new, cache_kt, cache_vt, cache_ft, layer):
    bsz, n_pages = page_table.shape
    row = lambda dt_shape: pl.BlockSpec((None,) + dt_shape, lambda b, pt: (b, 0, 0))
    kv_specs = [pl.BlockSpec((None, None, D_A, PAGE_SIZE), functools.partial(
        lambda b, pt, j: (pt[b, j], layer, 0, 0), j=j)) for j in range(n_pages)]
    f_specs = [pl.BlockSpec((None, None, H_A, PAGE_SIZE), functools.partial(
        lambda b, pt, j: (pt[b, j], layer, 0, 0), j=j)) for j in range(n_pages)]
    grid_spec = pltpu.PrefetchScalarGridSpec(
        num_scalar_prefetch=1,
        grid=(bsz,),
        in_specs=[row((1, D_A)), row((1, D_A)), row((1, D_A)), row((H_A, 1))] + kv_specs + kv_specs + f_specs,
        out_specs=row((1, D_A)),
    )
    return pl.pallas_call(
        functools.partial(_fox_decode_kernel, n_pages=n_pages),
        grid_spec=grid_spec,
        out_shape=jax.ShapeDtypeStruct((bsz, 1, D_A), BF16),
        compiler_params=_cparams("parallel"),
        name="fox_decode",
    )(page_table, q, k_new, v_new, f_new, *([cache_kt] * n_pages), *([cache_vt] * n_pages),
      *([cache_ft] * n_pages))


def _shifted_rows(cur, prev, n_shift):
    tm = cur.shape[0]
    row = lax.broadcasted_iota(jnp.int32, (tm, 1), 0)
    out = []
    for sft in range(1, n_shift + 1):
        r = pltpu.roll(cur, sft, 0)
        for i in range(sft):
            r = jnp.where(row == i, prev[SUBLANES - sft + i:SUBLANES - sft + i + 1, :], r)
        out.append(r)
    return out


def _ab_out_kernel(*refs, explicit_state, tiles_per_seq):
    if explicit_state:
        x_ref, g_ref, ya_ref, bg_ref, cx_ref, m1_ref, m2_ref, cw_ref, wo_ref, o_ref = refs
        cx = cx_ref[...]
        m1, m2 = m1_ref[...], m2_ref[...]
    else:
        x_ref, g_ref, ya_ref, bg_ref, cx_ref, prev_ref, cw_ref, wo_ref, o_ref = refs
        cx = cx_ref[...]
        first = (pl.program_id(0) % tiles_per_seq) == 0
        m1, m2 = _shifted_rows(cx, prev_ref[...] * jnp.where(first, 0.0, 1.0), CONV_B - 1)
    conv = m2 * cw_ref[0:1, :] + m1 * cw_ref[1:2, :] + cx * cw_ref[2:3, :]
    yb = bg_ref[...] * conv
    y = _dot(ya_ref[...], wo_ref[0:D_A, :]) + _dot(yb.astype(BF16), wo_ref[D_A:, :])
    o_ref[...] = x_ref[...] + _rms(y, g_ref[...])


def _ab_out(x, g, ya, bg, cx, conv_w, w_out, tm, seq_len=None, state=None):
    m = x.shape[0]
    explicit = state is not None
    if explicit:
        extra = [state[1], state[0]]
        extra_specs = [_row_spec(tm, D_B)] * 2
        tiles_per_seq = 1
    else:
        extra = [cx]
        per = tm // SUBLANES
        extra_specs = [pl.BlockSpec((SUBLANES, D_B), lambda i: (jnp.maximum(i * per - 1, 0), 0))]
        tiles_per_seq = seq_len // tm
    return pl.pallas_call(
        functools.partial(_ab_out_kernel, explicit_state=explicit, tiles_per_seq=tiles_per_seq),
        grid=(m // tm,),
        in_specs=[_row_spec(tm, D_MODEL), _full_spec((1, D_MODEL)), _row_spec(tm, D_A), _row_spec(tm, D_B),
                  _row_spec(tm, D_B)] + extra_specs + [_full_spec((CONV_B, D_B)), _full_spec((D_MODEL, D_MODEL))],
        out_specs=_row_spec(tm, D_MODEL),
        out_shape=jax.ShapeDtypeStruct((m, D_MODEL), F32),
        compiler_params=_cparams("parallel"),
        name="ab_out",
    )(x, g, ya, bg, cx, *extra, conv_w, w_out)


def _lru_gates(xc, wgi_ref, brg_ref, big_ref, lam_ref):
    r_parts, i_parts = [], []
    for n in range(N_BLK_C):
        ri = _dot(xc[:, n * BS_C:(n + 1) * BS_C].astype(BF16), wgi_ref[n])
        r_parts.append(ri[:, :BS_C])
        i_parts.append(ri[:, BS_C:])
    r = _sigmoid(jnp.concatenate(r_parts, axis=-1) + brg_ref[...])
    ig = _sigmoid(jnp.concatenate(i_parts, axis=-1) + big_ref[...])
    log_a = -LRU_C * r * _softplus(-lam_ref[...])
    a = jnp.exp(log_a)
    th = jnp.tanh(log_a)
    u = jnp.sqrt(-2.0 * th / (1.0 - th)) * (ig * xc)
    return a, u


def _c_mix_kernel(x_ref, g2_ref, g3_ref, win_ref, cw_ref, cb_ref, wgi_ref, brg_ref, big_ref, lam_ref, wout_ref,
                  o_ref, xr_tail_ref, h_tail_ref, prev_ref, h_ref, a_ref, u_ref):
    tm = x_ref.shape[0]
    ngrp = tm // SUBLANES

    @pl.when(pl.program_id(1) == 0)
    def _():
        prev_ref[...] = jnp.zeros_like(prev_ref)
        h_ref[...] = jnp.zeros_like(h_ref)

    x = x_ref[...]
    xn = _rms(x, g2_ref[...]).astype(BF16)
    gate = _dot(xn, win_ref[:, 0:D_RNN])
    xr = _dot(xn, win_ref[:, D_RNN:])
    m1, m2, m3 = _shifted_rows(xr, prev_ref[...], CONV_C - 1)
    xc = m3 * cw_ref[0:1, :] + m2 * cw_ref[1:2, :] + m1 * cw_ref[2:3, :] + xr * cw_ref[3:4, :] + cb_ref[...]
    a, u = _lru_gates(xc, wgi_ref, brg_ref, big_ref, lam_ref)

    a = a.reshape(ngrp, SUBLANES, D_RNN)
    u = u.reshape(ngrp, SUBLANES, D_RNN)
    sub = lax.broadcasted_iota(jnp.int32, (1, SUBLANES, 1), 1)
    d = 1
    while d < SUBLANES:
        ok = sub >= d
        a_sh = jnp.where(ok, pltpu.roll(a, d, 1), 1.0)
        u_sh = jnp.where(ok, pltpu.roll(u, d, 1), 0.0)
        u = a * u_sh + u
        a = a * a_sh
        d *= 2
    a_ref[...] = a
    u_ref[...] = u

    def group(i, h):
        hs = a_ref[i] * h + u_ref[i]
        u_ref[i] = hs
        return hs[SUBLANES - 1:SUBLANES, :]

    h_ref[...] = lax.fori_loop(0, ngrp, group, h_ref[...], unroll=4)
    hs = u_ref[...].reshape(tm, D_RNN)
    z = _gelu_tanh(gate) * hs
    o_ref[...] = x + _rms(_dot(z.astype(BF16), wout_ref[...]), g3_ref[...])
    prev_ref[...] = xr[tm - SUBLANES:, :]
    xr_tail_ref[...] = xr[tm - SUBLANES:, :]
    h_tail_ref[...] = hs[tm - SUBLANES:, :]


def _c_mix(x, g2, g3, w_in, conv_w, conv_b, wgi, b_rg, b_ig, lam, w_out, n_seq, tm):
    m = x.shape[0]
    nt = m // n_seq // tm
    rows = pl.BlockSpec((tm, D_MODEL), lambda b, t: (b * nt + t, 0))
    full = lambda shape: pl.BlockSpec(shape, lambda b, t: (0,) * len(shape))
    tail = pl.BlockSpec((None, SUBLANES, D_RNN), lambda b, t: (b, 0, 0))
    tail_shape = jax.ShapeDtypeStruct((n_seq, SUBLANES, D_RNN), F32)
    return pl.pallas_call(
        _c_mix_kernel,
        grid=(n_seq, nt),
        in_specs=[rows, full((1, D_MODEL)), full((1, D_MODEL)), full((D_MODEL, 2 * D_RNN)), full((CONV_C, D_RNN)),
                  full((1, D_RNN)), full((N_BLK_C, BS_C, 2 * BS_C)), full((1, D_RNN)), full((1, D_RNN)),
                  full((1, D_RNN)), full((D_RNN, D_MODEL))],
        out_specs=[rows, tail, tail],
        out_shape=[jax.ShapeDtypeStruct((m, D_MODEL), F32), tail_shape, tail_shape],
        scratch_shapes=[pltpu.VMEM((SUBLANES, D_RNN), F32), pltpu.VMEM((1, D_RNN), F32),
                        pltpu.VMEM((tm // SUBLANES, SUBLANES, D_RNN), F32),
                        pltpu.VMEM((tm // SUBLANES, SUBLANES, D_RNN), F32)],
        compiler_params=_cparams("parallel", "arbitrary"),
        name="c_mix",
    )(x, g2, g3, w_in, conv_w, conv_b, wgi, b_rg, b_ig, lam, w_out)


def _c_step_kernel(x_ref, g2_ref, g3_ref, win_ref, s0_ref, s1_ref, s2_ref, h0_ref, cw_ref, cb_ref, wgi_ref,
                   brg_ref, big_ref, lam_ref, wout_ref, o_ref, xr_ref, h_ref):
    x = x_ref[...]
    xn = _rms(x, g2_ref[...]).astype(BF16)
    gate = _dot(xn, win_ref[:, 0:D_RNN])
    xr = _dot(xn, win_ref[:, D_RNN:])
    xc = (s0_ref[...] * cw_ref[0:1, :] + s1_ref[...] * cw_ref[1:2, :] + s2_ref[...] * cw_ref[2:3, :]
          + xr * cw_ref[3:4, :] + cb_ref[...])
    a, u = _lru_gates(xc, wgi_ref, brg_ref, big_ref, lam_ref)
    h = a * h0_ref[...] + u
    z = _gelu_tanh(gate) * h
    o_ref[...] = x + _rms(_dot(z.astype(BF16), wout_ref[...]), g3_ref[...])
    xr_ref[...] = xr
    h_ref[...] = h


def _c_step(x, g2, g3, w_in, state, h0, conv_w, conv_b, wgi, b_rg, b_ig, lam, w_out):
    m = x.shape[0]
    rows = _row_spec(m, D_RNN)
    shape = jax.ShapeDtypeStruct((m, D_RNN), F32)
    return pl.pallas_call(
        _c_step_kernel,
        grid=(1,),
        in_specs=[rows, _full_spec((1, D_MODEL)), _full_spec((1, D_MODEL)), _full_spec((D_MODEL, 2 * D_RNN)),
                  rows, rows, rows, rows, _full_spec((CONV_C, D_RNN)), _full_spec((1, D_RNN)),
                  _full_spec((N_BLK_C, BS_C, 2 * BS_C)), _full_spec((1, D_RNN)), _full_spec((1, D_RNN)),
                  _full_spec((1, D_RNN)), _full_spec((D_RNN, D_MODEL))],
        out_specs=[rows, rows, rows],
        out_shape=[shape, shape, shape],
        compiler_params=_cparams("arbitrary"),
        name="c_step",
    )(x, g2, g3, w_in, state[0], state[1], state[2], h0, conv_w, conv_b, wgi, b_rg, b_ig, lam, w_out)


def kernel(x_prompt, x_sample, p_prompt, p_sample, cache_k, cache_v, cache_logf, state_conv_b, state_conv_c, state_h_c, page_table, norms, ffn1_wi, ffn1_wo, ffn2_wi, ffn2_wo, ple_w_pe, ple_w_pg, ab_w_in, ab_b_f, ab_conv_w, ab_w_out, c_w_in, c_conv_w, c_conv_b, c_w_rg, c_b_rg, c_w_ig, c_b_ig, c_lam, c_w_out):
    bp, s_len, _ = x_prompt.shape
    bs = x_sample.shape[0]
    depth = norms.shape[0]
    n_phys, n_ab = cache_k.shape[0], cache_k.shape[1]
    mp = bp * s_len
    tm_p = min(1024, mp)
    tm_x = min(512, s_len)

    xp = x_prompt.reshape(mp, D_MODEL)
    xs = x_sample.reshape(bs, D_MODEL)
    pp = p_prompt.reshape(depth, mp, D_PLE)
    psm = p_sample.reshape(depth, bs, D_PLE)
    cache_kt = jnp.transpose(cache_k, (0, 1, 3, 4, 2)).reshape(n_phys, n_ab, D_A, PAGE_SIZE)
    cache_vt = jnp.transpose(cache_v, (0, 1, 3, 4, 2)).reshape(n_phys, n_ab, D_A, PAGE_SIZE)
    cache_ft = jnp.swapaxes(cache_logf, 2, 3)
    w_pe = ple_w_pe.astype(BF16)
    w_pg = ple_w_pg.astype(BF16)

    kp_l, vp_l, fp_l, ks_l, vs_l, fs_l = [], [], [], [], [], []
    cbp_l, cbs_l, ccp_l, ccs_l, hp_l, hs_l = [], [], [], [], [], []
    for i in range(depth):
        g = norms[i].reshape(norms.shape[1], 1, D_MODEL)
        xp = _ffn_half(xp, g[0], g[1], ffn1_wi, ffn1_wo, i, tm_p)
        xs = _ffn_half(xs, g[0], g[1], ffn1_wi, ffn1_wo, i, bs)
        l = i // 2
        if i % 2 == 0:
            w_in = ab_w_in[l]
            wa = jnp.concatenate([w_in[:, :D_A], w_in[:, 3 * D_A + H_A:]], axis=1).astype(BF16)
            wkvf = jnp.pad(w_in[:, D_A:3 * D_A + H_A].T, ((0, BF16_ROWS - H_A), (0, 0))).astype(BF16)
            b_f = ab_b_f[l].reshape(H_A, 1)
            w_out = ab_w_out[l].astype(BF16)
            q, kt, vt, ktb, vtb, lft, bg, cx = _ab_proj(xp, g[2], wa, wkvf, b_f, bp, tm_x)
            nb = s_len // tm_x
            ck = _cumsum_lanes(lft).reshape(bp, N_PAIR, 2, nb, tm_x).transpose(0, 1, 3, 2, 4)
            ya = _fox_prompt(q, ktb, vtb, ck)
            xp = _ab_out(xp, g[3], ya, bg, cx, ab_conv_w[l], w_out, tm_x, seq_len=s_len)
            kp_l.append(kt)
            vp_l.append(vt)
            fp_l.append(lft)
            cbp_l.append(cx.reshape(bp, s_len, D_B)[:, s_len - (CONV_B - 1):])
            q, kt, vt, _, _, lft, bg, cx = _ab_proj(xs, g[2], wa, wkvf, b_f, 1, bs)
            ya = _fox_decode(page_table, q.reshape(bs, 1, D_A), kt[0].T.reshape(bs, 1, D_A),
                             vt[0].T.reshape(bs, 1, D_A), lft[0].T.reshape(bs, H_A, 1),
                             cache_kt, cache_vt, cache_ft, l)
            st = state_conv_b[:, l]
            xs = _ab_out(xs, g[3], ya.reshape(bs, D_A), bg, cx, ab_conv_w[l], w_out, bs,
                         state=(st[:, 0], st[:, 1]))
            ks_l.append(kt[0])
            vs_l.append(vt[0])
            fs_l.append(lft[0])
            cbs_l.append(jnp.stack([st[:, 1], cx], axis=1))
        else:
            w_in = c_w_in[l].astype(BF16)
            wgi = jnp.concatenate([c_w_rg[l], c_w_ig[l]], axis=-1).astype(BF16)
            row = lambda a: a.reshape(1, D_RNN)
            gate_args = (c_conv_w[l], row(c_conv_b[l]), wgi, row(c_b_rg[l]), row(c_b_ig[l]), row(c_lam[l]),
                         c_w_out[l].astype(BF16))
            xp, xr_tail, h_tail = _c_mix(xp, g[2], g[3], w_in, *gate_args, bp, tm_x)
            ccp_l.append(xr_tail[:, SUBLANES - (CONV_C - 1):])
            hp_l.append(h_tail[:, SUBLANES - 1])
            st = state_conv_c[:, l]
            xs, xr, h = _c_step(xs, g[2], g[3], w_in, (st[:, 0], st[:, 1], st[:, 2]), state_h_c[:, l], *gate_args)
            ccs_l.append(jnp.stack([st[:, 1], st[:, 2], xr], axis=1))
            hs_l.append(h)
        xp = _ffn_half(xp, g[4], g[5], ffn2_wi, ffn2_wo, i, tm_p)
        xs = _ffn_half(xs, g[4], g[5], ffn2_wi, ffn2_wo, i, bs)
        xp = _ple_add(xp, pp, g[6], w_pe, w_pg, i, tm_x)
        xs = _ple_add(xs, psm, g[6], w_pe, w_pg, i, bs)

    st = lambda lst: jnp.stack(lst, axis=1)
    kv_p = lambda lst: st(lst).reshape(bp, n_ab, H_A, HD_A, s_len).transpose(0, 1, 4, 2, 3)
    kv_s = lambda lst: jnp.stack(lst, axis=0).reshape(n_ab, 1, H_A, HD_A, bs).transpose(4, 0, 1, 2, 3)
    return (xp.reshape(bp, s_len, D_MODEL), xs.reshape(bs, 1, D_MODEL),
            kv_p(kp_l), kv_p(vp_l), st(fp_l).transpose(0, 1, 3, 2),
            kv_s(ks_l), kv_s(vs_l), jnp.stack(fs_l, axis=0).reshape(n_ab, 1, H_A, bs).transpose(3, 0, 1, 2),
            st(cbp_l), st(cbs_l), st(ccp_l), st(ccs_l), st(hp_l), st(hs_l))
```

```python
import functools

import jax
import jax.numpy as jnp
from jax import lax
from jax.experimental import pallas as pl
from jax.experimental.pallas import tpu as pltpu

F32 = jnp.float32
BF16 = jnp.bfloat16

D_MODEL = 1024
D_PLE = 256
H_A = 8
HD_A = 64
D_A = H_A * HD_A
D_B = D_MODEL - D_A
CONV_B = 3
D_RNN = D_MODEL
N_BLK_C = 8
BS_C = D_RNN // N_BLK_C
CONV_C = 4
LRU_C = 8.0
D_FF = 2816
EPS = 1e-6
NEG = -1e30
PAGE_SIZE = 128

LANES = 128
SUBLANES = 8
BF16_ROWS = 16
MXU_TILE = 256
FF_CHUNKS = ((0, 1024), (1024, 1024), (2048, 768))
ATTN_STRIP = 64
LOG2E = 1.4426950408889634
N_PAIR = D_A // LANES
KVF_ROWS = 2 * D_A + BF16_ROWS
VMEM_LIMIT = 56 * 1024 * 1024

_NT = (((1,), (1,)), ((), ()))


def _cparams(*sem):
    return pltpu.CompilerParams(dimension_semantics=sem, vmem_limit_bytes=VMEM_LIMIT)


def _rms(x, g):
    return x * lax.rsqrt(jnp.mean(x * x, axis=-1, keepdims=True) + EPS) * g


def _dot(a, b):
    return jnp.dot(a, b, preferred_element_type=F32)


def _dot_nt(a, b):
    return lax.dot_general(a, b, _NT, preferred_element_type=F32)


def _log_sigmoid(x):
    return jnp.minimum(x, 0.0) - jnp.log1p(jnp.exp(-jnp.abs(x)))


def _softplus(x):
    return jnp.maximum(x, 0.0) + jnp.log1p(jnp.exp(-jnp.abs(x)))


def _sigmoid(x):
    return 1.0 / (1.0 + jnp.exp(-x))


def _gelu_tanh(x):
    c = 0.7978845608028654
    return 0.5 * x * (1.0 + jnp.tanh(c * (x + 0.044715 * (x * x * x))))


def _row_spec(tm, d):
    return pl.BlockSpec((tm, d), lambda i: (i, 0))


def _full_spec(shape):
    nd = len(shape)
    return pl.BlockSpec(shape, lambda i: (0,) * nd)


def _ffn_kernel(*refs, with_ple):
    if with_ple:
        x_ref, gpre_ref, gpost_ref, wi_ref, wo_ref, p_ref, gple_ref, wpe_ref, wpg_ref, o_ref = refs
    else:
        x_ref, gpre_ref, gpost_ref, wi_ref, wo_ref, o_ref = refs
    x = x_ref[...]
    xn = _rms(x, gpre_ref[...]).astype(BF16)
    acc = None
    for c0, cw in FF_CHUNKS:
        g = _dot(xn, wi_ref[:, c0:c0 + cw])
        u = _dot(xn, wi_ref[:, D_FF + c0:D_FF + c0 + cw])
        h = (g * _sigmoid(g)) * u
        part = _dot(h.astype(BF16), wo_ref[c0:c0 + cw, :])
        acc = part if acc is None else acc + part
    x = x + 0.5 * _rms(acc, gpost_ref[...])
    if with_ple:
        gate = _sigmoid(_dot(_rms(x, gple_ref[...]).astype(BF16), wpg_ref[...]))
        x = x + _dot(p_ref[...].astype(BF16), wpe_ref[...]) * gate
    o_ref[...] = x


def _ffn_half(x, g_pre, g_post, wi, wo, layer, tm, ple=None):
    m = x.shape[0]
    resident = lambda shape: pl.BlockSpec((None,) + shape, lambda i: (layer, 0, 0), pipeline_mode=pl.Buffered(1))
    in_specs = [_row_spec(tm, D_MODEL), _full_spec((1, D_MODEL)), _full_spec((1, D_MODEL)),
                resident((D_MODEL, 2 * D_FF)), resident((D_FF, D_MODEL))]
    args = [x, g_pre, g_post, wi, wo]
    if ple is not None:
        p, g_ple, w_pe, w_pg = ple
        in_specs += [pl.BlockSpec((None, tm, D_PLE), lambda i: (layer, i, 0)), _full_spec((1, D_MODEL)),
                     resident((D_PLE, D_MODEL)), resident((D_MODEL, D_MODEL))]
        args += [p, g_ple, w_pe, w_pg]
    return pl.pallas_call(
        functools.partial(_ffn_kernel, with_ple=ple is not None),
        grid=(m // tm,),
        in_specs=in_specs,
        out_specs=_row_spec(tm, D_MODEL),
        out_shape=jax.ShapeDtypeStruct((m, D_MODEL), F32),
        compiler_params=_cparams("parallel"),
        name="ffn_half",
    )(*args)


def _ab_proj_kernel(x_ref, g_ref, wa_ref, wkvf_ref, bf_ref, q_ref, kt_ref, vt_ref, ktb_ref, vtb_ref, lft_ref,
                    bg_ref, cx_ref, *, q_scale):
    tm = x_ref.shape[0]
    xn = _rms(x_ref[...], g_ref[...]).astype(BF16)
    col = lambda c: _dot(xn, wa_ref[:, c * D_A:(c + 1) * D_A])
    q_ref[...] = (col(0) * q_scale).astype(BF16)
    bg_ref[...] = col(1)
    cx_ref[...] = col(2) * col(3)
    kvf = _dot_nt(wkvf_ref[...], xn)
    kt = kvf[0:D_A]
    vt = kvf[D_A:2 * D_A]
    kt_ref[...] = kt
    vt_ref[...] = vt
    ktb_ref[...] = kt.astype(BF16).reshape(N_PAIR, LANES, tm)
    vtb_ref[:, 0:HD_A, :] = vt.astype(BF16).reshape(H_A, HD_A, tm)
    ones_row = lax.broadcasted_iota(jnp.int32, (H_A, LANES - HD_A, tm), 1) == 0
    vtb_ref[:, HD_A:, :] = jnp.where(ones_row, 1.0, 0.0).astype(BF16)
    lft_ref[...] = _log_sigmoid(kvf[2 * D_A:2 * D_A + H_A] + bf_ref[...])


def _ab_proj(x, g, wa, wkvf, b_f, n_seq, tm, q_scale):
    m = x.shape[0]
    s_len = m // n_seq
    nt = s_len // tm
    rows = lambda d: pl.BlockSpec((tm, d), lambda b, t: (b * nt + t, 0))
    full = lambda shape: pl.BlockSpec(shape, lambda b, t: (0,) * len(shape))
    t_spec = lambda r: pl.BlockSpec((None, r, tm), lambda b, t: (b, 0, t))
    tile_spec = lambda n: pl.BlockSpec((None, n, None, LANES, tm), lambda b, t: (b, 0, t, 0, 0))
    wide = lambda dt: jax.ShapeDtypeStruct((m, D_A), dt)
    t_shape = lambda r: jax.ShapeDtypeStruct((n_seq, r, s_len), F32)
    tile_shape = lambda n: jax.ShapeDtypeStruct((n_seq, n, nt, LANES, tm), BF16)
    return pl.pallas_call(
        functools.partial(_ab_proj_kernel, q_scale=q_scale),
        grid=(n_seq, nt),
        in_specs=[rows(D_MODEL), full((1, D_MODEL)), full((D_MODEL, 4 * D_A)), full((KVF_ROWS, D_MODEL)),
                  full((H_A, 1))],
        out_specs=[rows(D_A), t_spec(D_A), t_spec(D_A), tile_spec(N_PAIR), tile_spec(H_A), t_spec(H_A),
                   rows(D_B), rows(D_B)],
        out_shape=[wide(BF16), t_shape(D_A), t_shape(D_A), tile_shape(N_PAIR), tile_shape(H_A), t_shape(H_A),
                   wide(F32), wide(F32)],
        compiler_params=_cparams("parallel", "parallel"),
        name="ab_proj",
    )(x, g, wa, wkvf, b_f)


def _cumsum_kernel(x_ref, o_ref):
    x = x_ref[...]
    n = x.shape[1]
    pos = lax.broadcasted_iota(jnp.int32, (1, n), 1)
    d = 1
    while d < n:
        x = x + jnp.where(pos >= d, pltpu.roll(x, d, 1), 0.0)
        d *= 2
    o_ref[...] = x * LOG2E


def _cumsum_lanes(x):
    spec = pl.BlockSpec((None,) + x.shape[1:], lambda b: (b, 0, 0))
    return pl.pallas_call(
        _cumsum_kernel,
        grid=(x.shape[0],),
        in_specs=[spec],
        out_specs=spec,
        out_shape=jax.ShapeDtypeStruct(x.shape, F32),
        compiler_params=_cparams("parallel"),
        name="logf_cumsum",
    )(x)


def _fox_prompt_kernel(q_ref, kt_ref, vt_ref, ck_ref, o_ref):
    qi = pl.program_id(2)
    t = q_ref.shape[0]
    lane = lax.broadcasted_iota(jnp.int32, (1, LANES), 1)
    q = q_ref[...]
    zero = jnp.zeros_like(q)
    qs = (jnp.where(lane < HD_A, q, zero), jnp.where(lane >= HD_A, q, zero))
    def tile(j, state, diagonal):
        kt = kt_ref[j]
        ck = ck_ref[j]
        out = []
        scores = [_dot(qs[e], kt) - ck[e:e + 1, :] for e in range(2)]
        for e in range(2):
            m_prev, acc_prev = state[e]
            s = scores[e]
            if diagonal:
                row = lax.broadcasted_iota(jnp.int32, (t, 1), 0)
                colk = lax.broadcasted_iota(jnp.int32, (1, t), 1)
                s = jnp.where(colk <= row, s, NEG)
            m_new = jnp.maximum(m_prev, jnp.max(s, axis=-1, keepdims=True))
            p = jnp.exp2(s - m_new).astype(BF16)
            acc = jnp.exp2(m_prev - m_new) * acc_prev + _dot_nt(p, vt_ref[e, j])
            out.append((m_new, acc))
        return tuple(out)

    init = tuple((jnp.full((t, 1), NEG, F32), jnp.zeros((t, LANES), F32)) for _ in range(2))
    state = lax.fori_loop(0, qi, lambda j, st: tile(j, st, False), init)
    (_, a0), (_, a1) = tile(qi, state, True)
    o0 = a0[:, 0:HD_A] / a0[:, HD_A:HD_A + 1]
    o1 = a1[:, 0:HD_A] / a1[:, HD_A:HD_A + 1]
    o_ref[...] = jnp.concatenate([o0, o1], axis=-1).astype(BF16)


def _fox_prompt(q, ktb, vtb, ck):
    n_seq, _, nb, _, t = ktb.shape
    return pl.pallas_call(
        _fox_prompt_kernel,
        grid=(n_seq, N_PAIR, nb),
        in_specs=[pl.BlockSpec((t, LANES), lambda b, h, i: (b * nb + i, h)),
                  pl.BlockSpec((None, None, nb, LANES, t), lambda b, h, i: (b, h, 0, 0, 0)),
                  pl.BlockSpec((None, 2, nb, LANES, t), lambda b, h, i: (b, h, 0, 0, 0)),
                  pl.BlockSpec((None, None, nb, 2, t), lambda b, h, i: (b, h, 0, 0, 0))],
        out_specs=pl.BlockSpec((t, LANES), lambda b, h, i: (b * nb + i, h)),
        out_shape=jax.ShapeDtypeStruct(q.shape, BF16),
        compiler_params=_cparams("parallel", "parallel", "parallel"),
        name="fox_prompt",
    )(q, ktb, vtb, ck)


def _fox_decode_kernel(pt_ref, q_ref, kn_ref, vn_ref, fn_ref, *refs, n_pages):
    del pt_ref
    k_refs = refs[:n_pages]
    v_refs = refs[n_pages:2 * n_pages]
    f_refs = refs[2 * n_pages:3 * n_pages]
    o_ref = refs[3 * n_pages]
    ps = PAGE_SIZE

    lane = lax.broadcasted_iota(jnp.int32, (H_A, D_A), 1)
    hrow = lax.broadcasted_iota(jnp.int32, (H_A, D_A), 0)
    own = (lane // HD_A) == hrow
    qbd_f = jnp.where(own, q_ref[...].astype(F32), 0.0)
    qbd = qbd_f.astype(BF16)

    pos = lax.broadcasted_iota(jnp.int32, (H_A, ps), 1)
    carry = fn_ref[...]
    bias = [None] * n_pages
    for j in reversed(range(n_pages)):
        lf = f_refs[j][...]
        x = lf
        d = 1
        while d < ps:
            x = x + jnp.where(pos + d < ps, pltpu.roll(x, ps - d, 1), 0.0)
            d *= 2
        bias[j] = (x - lf) + carry
        carry = carry + x[:, 0:1]

    kn = kn_ref[...].astype(BF16).astype(F32)
    s_new = jnp.sum(qbd_f * kn, axis=-1, keepdims=True)
    scores = []
    m = s_new
    for j in range(n_pages):
        s = _dot(qbd, k_refs[j][...].astype(BF16)) + bias[j]
        scores.append(s)
        m = jnp.maximum(m, jnp.max(s, axis=-1, keepdims=True))
    p_new = jnp.exp(s_new - m)
    l = p_new
    vn = vn_ref[...].astype(BF16).astype(F32)
    acc = p_new.astype(BF16).astype(F32) * vn
    for j in range(n_pages):
        p = jnp.exp(scores[j] - m)
        l = l + jnp.sum(p, axis=-1, keepdims=True)
        acc = acc + _dot_nt(p.astype(BF16), v_refs[j][...].astype(BF16))
    o = jnp.where(own, acc / l, 0.0)
    o_ref[...] = jnp.sum(o, axis=0, keepdims=True).astype(BF16)


def _fox_decode(page_table, q, k_new, v_new, f_new, cache_kt, cache_vt, cache_ft, layer):
    bsz, n_pages = page_table.shape
    row = lambda dt_shape: pl.BlockSpec((None,) + dt_shape, lambda b, pt: (b, 0, 0))
    kv_specs = [pl.BlockSpec((None, None, D_A, PAGE_SIZE), functools.partial(
        lambda b, pt, j: (pt[b, j], layer, 0, 0), j=j)) for j in range(n_pages)]
    f_specs = [pl.BlockSpec((None, None, H_A, PAGE_SIZE), functools.partial(
        lambda b, pt, j: (pt[b, j], layer, 0, 0), j=j)) for j in range(n_pages)]
    grid_spec = pltpu.PrefetchScalarGridSpec(
        num_scalar_prefetch=1,
        grid=(bsz,),
        in_specs=[row((1, D_A)), row((1, D_A)), row((1, D_A)), row((H_A, 1))] + kv_specs + kv_specs + f_specs,
        out_specs=row((1, D_A)),
    )
    return pl.pallas_call(
        functools.partial(_fox_decode_kernel, n_pages=n_pages),
        grid_spec=grid_spec,
        out_shape=jax.ShapeDtypeStruct((bsz, 1, D_A), BF16),
        compiler_params=_cparams("parallel"),
        name="fox_decode",
    )(page_table, q, k_new, v_new, f_new, *([cache_kt] * n_pages), *([cache_vt] * n_pages),
      *([cache_ft] * n_pages))


def _shifted_rows(cur, prev, n_shift):
    tm = cur.shape[0]
    row = lax.broadcasted_iota(jnp.int32, (tm, 1), 0)
    out = []
    for sft in range(1, n_shift + 1):
        r = pltpu.roll(cur, sft, 0)
        for i in range(sft):
            r = jnp.where(row == i, prev[SUBLANES - sft + i:SUBLANES - sft + i + 1, :], r)
        out.append(r)
    return out


def _ab_out_kernel(*refs, explicit_state, tiles_per_seq):
    if explicit_state:
        x_ref, g_ref, ya_ref, bg_ref, cx_ref, m1_ref, m2_ref, cw_ref, wo_ref, o_ref = refs
        cx = cx_ref[...]
        m1, m2 = m1_ref[...], m2_ref[...]
    else:
        x_ref, g_ref, ya_ref, bg_ref, cx_ref, prev_ref, cw_ref, wo_ref, o_ref = refs
        cx = cx_ref[...]
        first = (pl.program_id(0) % tiles_per_seq) == 0
        m1, m2 = _shifted_rows(cx, prev_ref[...] * jnp.where(first, 0.0, 1.0), CONV_B - 1)
    conv = m2 * cw_ref[0:1, :] + m1 * cw_ref[1:2, :] + cx * cw_ref[2:3, :]
    yb = bg_ref[...] * conv
    y = _dot(ya_ref[...], wo_ref[0:D_A, :]) + _dot(yb.astype(BF16), wo_ref[D_A:, :])
    o_ref[...] = x_ref[...] + _rms(y, g_ref[...])


def _ab_out(x, g, ya, bg, cx, conv_w, w_out, tm, seq_len=None, state=None):
    m = x.shape[0]
    explicit = state is not None
    if explicit:
        extra = [state[1], state[0]]
        extra_specs = [_row_spec(tm, D_B)] * 2
        tiles_per_seq = 1
    else:
        extra = [cx]
        per = tm // SUBLANES
        extra_specs = [pl.BlockSpec((SUBLANES, D_B), lambda i: (jnp.maximum(i * per - 1, 0), 0))]
        tiles_per_seq = seq_len // tm
    return pl.pallas_call(
        functools.partial(_ab_out_kernel, explicit_state=explicit, tiles_per_seq=tiles_per_seq),
        grid=(m // tm,),
        in_specs=[_row_spec(tm, D_MODEL), _full_spec((1, D_MODEL)), _row_spec(tm, D_A), _row_spec(tm, D_B),
                  _row_spec(tm, D_B)] + extra_specs + [_full_spec((CONV_B, D_B)), _full_spec((D_MODEL, D_MODEL))],
        out_specs=_row_spec(tm, D_MODEL),
        out_shape=jax.ShapeDtypeStruct((m, D_MODEL), F32),
        compiler_params=_cparams("parallel"),
        name="ab_out",
    )(x, g, ya, bg, cx, *extra, conv_w, w_out)


def _lru_gates(xc, wgi_ref, brg_ref, big_ref, lam_ref):
    r_parts, i_parts = [], []
    for n in range(N_BLK_C):
        ri = _dot(xc[:, n * BS_C:(n + 1) * BS_C].astype(BF16), wgi_ref[n])
        r_parts.append(ri[:, :BS_C])
        i_parts.append(ri[:, BS_C:])
    r = _sigmoid(jnp.concatenate(r_parts, axis=-1) + brg_ref[...])
    ig = _sigmoid(jnp.concatenate(i_parts, axis=-1) + big_ref[...])
    log_a = -LRU_C * r * _softplus(-lam_ref[...])
    a = jnp.exp(log_a)
    th = jnp.tanh(log_a)
    u = jnp.sqrt(-2.0 * th / (1.0 - th)) * (ig * xc)
    return a, u


def _c_mix_kernel(x_ref, g2_ref, g3_ref, win_ref, cw_ref, cb_ref, wgi_ref, brg_ref, big_ref, lam_ref, wout_ref,
                  o_ref, xr_tail_ref, h_tail_ref, prev_ref, h_ref, a_ref, u_ref):
    tm = x_ref.shape[0]
    ngrp = tm // SUBLANES

    @pl.when(pl.program_id(1) == 0)
    def _():
        prev_ref[...] = jnp.zeros_like(prev_ref)
        h_ref[...] = jnp.zeros_like(h_ref)

    x = x_ref[...]
    xn = _rms(x, g2_ref[...]).astype(BF16)
    gate = _dot(xn, win_ref[:, 0:D_RNN])
    xr = _dot(xn, win_ref[:, D_RNN:])
    m1, m2, m3 = _shifted_rows(xr, prev_ref[...], CONV_C - 1)
    xc = m3 * cw_ref[0:1, :] + m2 * cw_ref[1:2, :] + m1 * cw_ref[2:3, :] + xr * cw_ref[3:4, :] + cb_ref[...]
    a, u = _lru_gates(xc, wgi_ref, brg_ref, big_ref, lam_ref)

    a = a.reshape(ngrp, SUBLANES, D_RNN)
    u = u.reshape(ngrp, SUBLANES, D_RNN)
    sub = lax.broadcasted_iota(jnp.int32, (1, SUBLANES, 1), 1)
    d = 1
    while d < SUBLANES:
        ok = sub >= d
        a_sh = jnp.where(ok, pltpu.roll(a, d, 1), 1.0)
        u_sh = jnp.where(ok, pltpu.roll(u, d, 1), 0.0)
        u = a * u_sh + u
        a = a * a_sh
        d *= 2
    a_ref[...] = a
    u_ref[...] = u

    def group(i, h):
        hs = a_ref[i] * h + u_ref[i]
        u_ref[i] = hs
        return hs[SUBLANES - 1:SUBLANES, :]

    h_ref[...] = lax.fori_loop(0, ngrp, group, h_ref[...], unroll=4)
    hs = u_ref[...].reshape(tm, D_RNN)
    z = _gelu_tanh(gate) * hs
    o_ref[...] = x + _rms(_dot(z.astype(BF16), wout_ref[...]), g3_ref[...])
    prev_ref[...] = xr[tm - SUBLANES:, :]
    xr_tail_ref[...] = xr[tm - SUBLANES:, :]
    h_tail_ref[...] = hs[tm - SUBLANES:, :]


def _c_mix(x, g2, g3, w_in, conv_w, conv_b, wgi, b_rg, b_ig, lam, w_out, n_seq, tm):
    m = x.shape[0]
    nt = m // n_seq // tm
    rows = pl.BlockSpec((tm, D_MODEL), lambda b, t: (b * nt + t, 0))
    full = lambda shape: pl.BlockSpec(shape, lambda b, t: (0,) * len(shape))
    tail = pl.BlockSpec((None, SUBLANES, D_RNN), lambda b, t: (b, 0, 0))
    tail_shape = jax.ShapeDtypeStruct((n_seq, SUBLANES, D_RNN), F32)
    return pl.pallas_call(
        _c_mix_kernel,
        grid=(n_seq, nt),
        in_specs=[rows, full((1, D_MODEL)), full((1, D_MODEL)), full((D_MODEL, 2 * D_RNN)), full((CONV_C, D_RNN)),
                  full((1, D_RNN)), full((N_BLK_C, BS_C, 2 * BS_C)), full((1, D_RNN)), full((1, D_RNN)),
                  full((1, D_RNN)), full((D_RNN, D_MODEL))],
        out_specs=[rows, tail, tail],
        out_shape=[jax.ShapeDtypeStruct((m, D_MODEL), F32), tail_shape, tail_shape],
        scratch_shapes=[pltpu.VMEM((SUBLANES, D_RNN), F32), pltpu.VMEM((1, D_RNN), F32),
                        pltpu.VMEM((tm // SUBLANES, SUBLANES, D_RNN), F32),
                        pltpu.VMEM((tm // SUBLANES, SUBLANES, D_RNN), F32)],
        compiler_params=_cparams("parallel", "arbitrary"),
        name="c_mix",
    )(x, g2, g3, w_in, conv_w, conv_b, wgi, b_rg, b_ig, lam, w_out)


def _c_step_kernel(x_ref, g2_ref, g3_ref, win_ref, s0_ref, s1_ref, s2_ref, h0_ref, cw_ref, cb_ref, wgi_ref,
                   brg_ref, big_ref, lam_ref, wout_ref, o_ref, xr_ref, h_ref):
    x = x_ref[...]
    xn = _rms(x, g2_ref[...]).astype(BF16)
    gate = _dot(xn, win_ref[:, 0:D_RNN])
    xr = _dot(xn, win_ref[:, D_RNN:])
    xc = (s0_ref[...] * cw_ref[0:1, :] + s1_ref[...] * cw_ref[1:2, :] + s2_ref[...] * cw_ref[2:3, :]
          + xr * cw_ref[3:4, :] + cb_ref[...])
    a, u = _lru_gates(xc, wgi_ref, brg_ref, big_ref, lam_ref)
    h = a * h0_ref[...] + u
    z = _gelu_tanh(gate) * h
    o_ref[...] = x + _rms(_dot(z.astype(BF16), wout_ref[...]), g3_ref[...])
    xr_ref[...] = xr
    h_ref[...] = h


def _c_step(x, g2, g3, w_in, state, h0, conv_w, conv_b, wgi, b_rg, b_ig, lam, w_out):
    m = x.shape[0]
    rows = _row_spec(m, D_RNN)
    shape = jax.ShapeDtypeStruct((m, D_RNN), F32)
    return pl.pallas_call(
        _c_step_kernel,
        grid=(1,),
        in_specs=[rows, _full_spec((1, D_MODEL)), _full_spec((1, D_MODEL)), _full_spec((D_MODEL, 2 * D_RNN)),
                  rows, rows, rows, rows, _full_spec((CONV_C, D_RNN)), _full_spec((1, D_RNN)),
                  _full_spec((N_BLK_C, BS_C, 2 * BS_C)), _full_spec((1, D_RNN)), _full_spec((1, D_RNN)),
                  _full_spec((1, D_RNN)), _full_spec((D_RNN, D_MODEL))],
        out_specs=[rows, rows, rows],
        out_shape=[shape, shape, shape],
        compiler_params=_cparams("arbitrary"),
        name="c_step",
    )(x, g2, g3, w_in, state[0], state[1], state[2], h0, conv_w, conv_b, wgi, b_rg, b_ig, lam, w_out)


def kernel(x_prompt, x_sample, p_prompt, p_sample, cache_k, cache_v, cache_logf, state_conv_b, state_conv_c, state_h_c, page_table, norms, ffn1_wi, ffn1_wo, ffn2_wi, ffn2_wo, ple_w_pe, ple_w_pg, ab_w_in, ab_b_f, ab_conv_w, ab_w_out, c_w_in, c_conv_w, c_conv_b, c_w_rg, c_b_rg, c_w_ig, c_b_ig, c_lam, c_w_out):
    bp, s_len, _ = x_prompt.shape
    bs = x_sample.shape[0]
    depth = norms.shape[0]
    n_phys, n_ab = cache_k.shape[0], cache_k.shape[1]
    mp = bp * s_len
    tm_x = min(512, s_len)

    xp = x_prompt.reshape(mp, D_MODEL)
    xs = x_sample.reshape(bs, D_MODEL)
    pp = p_prompt.reshape(depth, mp, D_PLE)
    psm = p_sample.reshape(depth, bs, D_PLE)
    cache_kt = jnp.transpose(cache_k, (0, 1, 3, 4, 2)).reshape(n_phys, n_ab, D_A, PAGE_SIZE)
    cache_vt = jnp.transpose(cache_v, (0, 1, 3, 4, 2)).reshape(n_phys, n_ab, D_A, PAGE_SIZE)
    cache_ft = jnp.swapaxes(cache_logf, 2, 3)
    w_pe = ple_w_pe.astype(BF16)
    w_pg = ple_w_pg.astype(BF16)
    wi1, wo1 = ffn1_wi.astype(BF16), ffn1_wo.astype(BF16)
    wi2, wo2 = ffn2_wi.astype(BF16), ffn2_wo.astype(BF16)

    kp_l, vp_l, fp_l, ks_l, vs_l, fs_l = [], [], [], [], [], []
    cbp_l, cbs_l, ccp_l, ccs_l, hp_l, hs_l = [], [], [], [], [], []
    for i in range(depth):
        g = norms[i].reshape(norms.shape[1], 1, D_MODEL)
        xp = _ffn_half(xp, g[0], g[1], wi1, wo1, i, tm_x)
        xs = _ffn_half(xs, g[0], g[1], wi1, wo1, i, bs)
        l = i // 2
        if i % 2 == 0:
            w_in = ab_w_in[l]
            wa = jnp.concatenate([w_in[:, :D_A], w_in[:, 3 * D_A + H_A:]], axis=1).astype(BF16)
            wkvf = jnp.pad(w_in[:, D_A:3 * D_A + H_A].T, ((0, BF16_ROWS - H_A), (0, 0))).astype(BF16)
            b_f = ab_b_f[l].reshape(H_A, 1)
            w_out = ab_w_out[l].astype(BF16)
            q, kt, vt, ktb, vtb, lft, bg, cx = _ab_proj(xp, g[2], wa, wkvf, b_f, bp, tm_x, HD_A ** -0.5 * LOG2E)
            nb = s_len // tm_x
            ck = _cumsum_lanes(lft).reshape(bp, N_PAIR, 2, nb, tm_x).transpose(0, 1, 3, 2, 4)
            ya = _fox_prompt(q, ktb, vtb, ck)
            xp = _ab_out(xp, g[3], ya, bg, cx, ab_conv_w[l], w_out, tm_x, seq_len=s_len)
            kp_l.append(kt)
            vp_l.append(vt)
            fp_l.append(lft)
            cbp_l.append(cx.reshape(bp, s_len, D_B)[:, s_len - (CONV_B - 1):])
            q, kt, vt, _, _, lft, bg, cx = _ab_proj(xs, g[2], wa, wkvf, b_f, 1, bs, HD_A ** -0.5)
            ya = _fox_decode(page_table, q.reshape(bs, 1, D_A), kt[0].T.reshape(bs, 1, D_A),
                             vt[0].T.reshape(bs, 1, D_A), lft[0].T.reshape(bs, H_A, 1),
                             cache_kt, cache_vt, cache_ft, l)
            st = state_conv_b[:, l]
            xs = _ab_out(xs, g[3], ya.reshape(bs, D_A), bg, cx, ab_conv_w[l], w_out, bs,
                         state=(st[:, 0], st[:, 1]))
            ks_l.append(kt[0])
            vs_l.append(vt[0])
            fs_l.append(lft[0])
            cbs_l.append(jnp.stack([st[:, 1], cx], axis=1))
        else:
            w_in = c_w_in[l].astype(BF16)
            wgi = jnp.concatenate([c_w_rg[l], c_w_ig[l]], axis=-1).astype(BF16)
            row = lambda a: a.reshape(1, D_RNN)
            gate_args = (c_conv_w[l], row(c_conv_b[l]), wgi, row(c_b_rg[l]), row(c_b_ig[l]), row(c_lam[l]),
                         c_w_out[l].astype(BF16))
            xp, xr_tail, h_tail = _c_mix(xp, g[2], g[3], w_in, *gate_args, bp, tm_x)
            ccp_l.append(xr_tail[:, SUBLANES - (CONV_C - 1):])
            hp_l.append(h_tail[:, SUBLANES - 1])
            st = state_conv_c[:, l]
            xs, xr, h = _c_step(xs, g[2], g[3], w_in, (st[:, 0], st[:, 1], st[:, 2]), state_h_c[:, l], *gate_args)
            ccs_l.append(jnp.stack([st[:, 1], st[:, 2], xr], axis=1))
            hs_l.append(h)
        xp = _ffn_half(xp, g[4], g[5], wi2, wo2, i, tm_x, ple=(pp, g[6], w_pe, w_pg))
        xs = _ffn_half(xs, g[4], g[5], wi2, wo2, i, bs, ple=(psm, g[6], w_pe, w_pg))

    st = lambda lst: jnp.stack(lst, axis=1)
    kv_p = lambda lst: st(lst).reshape(bp, n_ab, H_A, HD_A, s_len).transpose(0, 1, 4, 2, 3)
    kv_s = lambda lst: jnp.stack(lst, axis=0).reshape(n_ab, 1, H_A, HD_A, bs).transpose(4, 0, 1, 2, 3)
    return (xp.reshape(bp, s_len, D_MODEL), xs.reshape(bs, 1, D_MODEL),
            kv_p(kp_l), kv_p(vp_l), st(fp_l).transpose(0, 1, 3, 2),
            kv_s(ks_l), kv_s(vs_l), jnp.stack(fs_l, axis=0).reshape(n_ab, 1, H_A, bs).transpose(3, 0, 1, 2),
            st(cbp_l), st(cbs_l), st(ccp_l), st(ccs_l), st(hp_l), st(hs_l))
```

```python
import functools

import jax
import jax.numpy as jnp
from jax import lax
from jax.experimental import pallas as pl
from jax.experimental.pallas import tpu as pltpu

F32 = jnp.float32
BF16 = jnp.bfloat16

D_MODEL = 1024
D_PLE = 256
H_A = 8
HD_A = 64
D_A = H_A * HD_A
D_B = D_MODEL - D_A
CONV_B = 3
D_RNN = D_MODEL
N_BLK_C = 8
BS_C = D_RNN // N_BLK_C
CONV_C = 4
LRU_C = 8.0
D_FF = 2816
EPS = 1e-6
NEG = -1e30
PAGE_SIZE = 128

LANES = 128
SUBLANES = 8
BF16_ROWS = 16
FF_CHUNKS = ((0, 1024), (1024, 1024), (2048, 768))
LOG2E = 1.4426950408889634
DECODE_SEQS = 2
N_PAIR = D_A // LANES
KVF_ROWS = 2 * D_A + BF16_ROWS
VMEM_LIMIT = 56 * 1024 * 1024

_NT = (((1,), (1,)), ((), ()))


def _cparams(*sem):
    return pltpu.CompilerParams(dimension_semantics=sem, vmem_limit_bytes=VMEM_LIMIT)


def _rms(x, g):
    return x * lax.rsqrt(jnp.mean(x * x, axis=-1, keepdims=True) + EPS) * g


def _dot(a, b):
    return jnp.dot(a, b, preferred_element_type=F32)


def _dot_nt(a, b):
    return lax.dot_general(a, b, _NT, preferred_element_type=F32)


def _log_sigmoid(x):
    return jnp.minimum(x, 0.0) - jnp.log1p(jnp.exp(-jnp.abs(x)))


def _softplus(x):
    return jnp.maximum(x, 0.0) + jnp.log1p(jnp.exp(-jnp.abs(x)))


def _sigmoid(x):
    return 1.0 / (1.0 + jnp.exp(-x))


def _gelu_tanh(x):
    c = 0.7978845608028654
    return 0.5 * x * (1.0 + jnp.tanh(c * (x + 0.044715 * (x * x * x))))


def _row_spec(tm, d):
    return pl.BlockSpec((tm, d), lambda i: (i, 0))


def _full_spec(shape):
    nd = len(shape)
    return pl.BlockSpec(shape, lambda i: (0,) * nd)


def _shifted_rows(cur, prev, n_shift):
    sub = lax.broadcasted_iota(jnp.int32, (SUBLANES, 1), 0)
    out = []
    for sft in range(1, n_shift + 1):
        r = pltpu.roll(cur, sft, 0)
        top = r[0:SUBLANES, :]
        for i in range(sft):
            top = jnp.where(sub == i, prev[SUBLANES - sft + i:SUBLANES - sft + i + 1, :], top)
        out.append(jnp.concatenate([top, r[SUBLANES:, :]], axis=0))
    return out


def _short_conv_out(x, ya, bg, cx, m1, m2, cw_ref, wo_ref, g):
    conv = m2 * cw_ref[0:1, :] + m1 * cw_ref[1:2, :] + cx * cw_ref[2:3, :]
    yb = bg * conv
    y = _dot(ya, wo_ref[0:D_A, :]) + _dot(yb.astype(BF16), wo_ref[D_A:, :])
    return x + _rms(y, g)


def _ffn_kernel(*refs, with_mix, with_ple, tiles_per_seq):
    refs = list(refs)
    x = refs.pop(0)[...]
    if with_mix:
        gmix_ref, ya_ref, bg_ref, cx_ref, prev_ref, cw_ref, wmix_ref = refs[:7]
        del refs[:7]
        cx = cx_ref[...]
        first = (pl.program_id(0) % tiles_per_seq) == 0
        m1, m2 = _shifted_rows(cx, prev_ref[...] * jnp.where(first, 0.0, 1.0), CONV_B - 1)
        x = _short_conv_out(x, ya_ref[...], bg_ref[...], cx, m1, m2, cw_ref, wmix_ref, gmix_ref[...])
    gpre_ref, gpost_ref, wi_ref, wo_ref = refs[:4]
    del refs[:4]
    xn = _rms(x, gpre_ref[...]).astype(BF16)
    acc = None
    for c0, cw in FF_CHUNKS:
        g = _dot(xn, wi_ref[:, c0:c0 + cw])
        u = _dot(xn, wi_ref[:, D_FF + c0:D_FF + c0 + cw])
        h = (g * _sigmoid(g)) * u
        part = _dot(h.astype(BF16), wo_ref[c0:c0 + cw, :])
        acc = part if acc is None else acc + part
    x = x + 0.5 * _rms(acc, gpost_ref[...])
    if with_ple:
        p_ref, gple_ref, wpe_ref, wpg_ref = refs[:4]
        del refs[:4]
        gate = _sigmoid(_dot(_rms(x, gple_ref[...]).astype(BF16), wpg_ref[...]))
        x = x + _dot(p_ref[...].astype(BF16), wpe_ref[...]) * gate
    o_ref, = refs
    o_ref[...] = x


def _ffn_half(x, g_pre, g_post, wi, wo, layer, tm, ple=None, mix=None):
    m = x.shape[0]
    resident = lambda shape: pl.BlockSpec((None,) + shape, lambda i: (layer, 0, 0), pipeline_mode=pl.Buffered(1))
    in_specs = [_row_spec(tm, D_MODEL)]
    args = [x]
    tiles_per_seq = 1
    if mix is not None:
        g_mix, ya, bg, cx, conv_w, w_out, seq_len = mix
        tiles_per_seq = seq_len // tm
        per = tm // SUBLANES
        in_specs += [_full_spec((1, D_MODEL)), _row_spec(tm, D_A), _row_spec(tm, D_B), _row_spec(tm, D_B),
                     pl.BlockSpec((SUBLANES, D_B), lambda i: (jnp.maximum(i * per - 1, 0), 0)),
                     _full_spec((CONV_B, D_B)), _full_spec((D_MODEL, D_MODEL))]
        args += [g_mix, ya, bg, cx, cx, conv_w, w_out]
    in_specs += [_full_spec((1, D_MODEL)), _full_spec((1, D_MODEL)),
                 resident((D_MODEL, 2 * D_FF)), resident((D_FF, D_MODEL))]
    args += [g_pre, g_post, wi, wo]
    if ple is not None:
        p, g_ple, w_pe, w_pg = ple
        in_specs += [pl.BlockSpec((None, tm, D_PLE), lambda i: (layer, i, 0)), _full_spec((1, D_MODEL)),
                     resident((D_PLE, D_MODEL)), resident((D_MODEL, D_MODEL))]
        args += [p, g_ple, w_pe, w_pg]
    return pl.pallas_call(
        functools.partial(_ffn_kernel, with_mix=mix is not None, with_ple=ple is not None,
                          tiles_per_seq=tiles_per_seq),
        grid=(m // tm,),
        in_specs=in_specs,
        out_specs=_row_spec(tm, D_MODEL),
        out_shape=jax.ShapeDtypeStruct((m, D_MODEL), F32),
        compiler_params=_cparams("parallel"),
        name="ffn_half",
    )(*args)


def _ab_proj_kernel(x_ref, g_ref, wa_ref, wkvf_ref, bf_ref, q_ref, kt_ref, vt_ref, ktb_ref, vtb_ref, lft_ref,
                    bg_ref, cx_ref, *, q_scale):
    tm = x_ref.shape[0]
    xn = _rms(x_ref[...], g_ref[...]).astype(BF16)
    col = lambda c: _dot(xn, wa_ref[:, c * D_A:(c + 1) * D_A])
    q_ref[...] = (col(0) * q_scale).astype(BF16)
    bg_ref[...] = col(1)
    cx_ref[...] = col(2) * col(3)
    kvf = _dot_nt(wkvf_ref[...], xn)
    kt = kvf[0:D_A]
    vt = kvf[D_A:2 * D_A]
    kt_ref[...] = kt
    vt_ref[...] = vt
    ktb_ref[...] = kt.astype(BF16).reshape(N_PAIR, LANES, tm)
    vtb_ref[:, 0:HD_A, :] = vt.astype(BF16).reshape(H_A, HD_A, tm)
    ones_row = lax.broadcasted_iota(jnp.int32, (H_A, LANES - HD_A, tm), 1) == 0
    vtb_ref[:, HD_A:, :] = jnp.where(ones_row, 1.0, 0.0).astype(BF16)
    lft_ref[...] = _log_sigmoid(kvf[2 * D_A:2 * D_A + H_A] + bf_ref[...])


def _ab_proj(x, g, wa, wkvf, b_f, n_seq, tm, q_scale):
    m = x.shape[0]
    s_len = m // n_seq
    nt = s_len // tm
    rows = lambda d: pl.BlockSpec((tm, d), lambda b, t: (b * nt + t, 0))
    full = lambda shape: pl.BlockSpec(shape, lambda b, t: (0,) * len(shape))
    t_spec = lambda r: pl.BlockSpec((None, r, tm), lambda b, t: (b, 0, t))
    tile_spec = lambda n: pl.BlockSpec((None, n, None, LANES, tm), lambda b, t: (b, 0, t, 0, 0))
    wide = lambda dt: jax.ShapeDtypeStruct((m, D_A), dt)
    t_shape = lambda r: jax.ShapeDtypeStruct((n_seq, r, s_len), F32)
    tile_shape = lambda n: jax.ShapeDtypeStruct((n_seq, n, nt, LANES, tm), BF16)
    return pl.pallas_call(
        functools.partial(_ab_proj_kernel, q_scale=q_scale),
        grid=(n_seq, nt),
        in_specs=[rows(D_MODEL), full((1, D_MODEL)), full((D_MODEL, 4 * D_A)), full((KVF_ROWS, D_MODEL)),
                  full((H_A, 1))],
        out_specs=[rows(D_A), t_spec(D_A), t_spec(D_A), tile_spec(N_PAIR), tile_spec(H_A), t_spec(H_A),
                   rows(D_B), rows(D_B)],
        out_shape=[wide(BF16), t_shape(D_A), t_shape(D_A), tile_shape(N_PAIR), tile_shape(H_A), t_shape(H_A),
                   wide(F32), wide(F32)],
        compiler_params=_cparams("parallel", "parallel"),
        name="ab_proj",
    )(x, g, wa, wkvf, b_f)


def _cumsum_kernel(x_ref, o_ref):
    x = x_ref[...]
    n = x.shape[1]
    pos = lax.broadcasted_iota(jnp.int32, (1, n), 1)
    d = 1
    while d < n:
        x = x + jnp.where(pos >= d, pltpu.roll(x, d, 1), 0.0)
        d *= 2
    o_ref[...] = x * LOG2E


def _cumsum_lanes(x):
    spec = pl.BlockSpec((None,) + x.shape[1:], lambda b: (b, 0, 0))
    return pl.pallas_call(
        _cumsum_kernel,
        grid=(x.shape[0],),
        in_specs=[spec],
        out_specs=spec,
        out_shape=jax.ShapeDtypeStruct(x.shape, F32),
        compiler_params=_cparams("parallel"),
        name="logf_cumsum",
    )(x)


def _fox_prompt_kernel(q_ref, kt_ref, vt_ref, ck_ref, o_ref):
    qi = pl.program_id(2)
    t = q_ref.shape[0]
    lane = lax.broadcasted_iota(jnp.int32, (1, LANES), 1)
    q = q_ref[...]
    zero = jnp.zeros_like(q)
    qs = (jnp.where(lane < HD_A, q, zero), jnp.where(lane >= HD_A, q, zero))

    def tile(j, state, diagonal):
        kt = kt_ref[j]
        ck = ck_ref[j]
        out = []
        scores = [_dot(qs[e], kt) - ck[e:e + 1, :] for e in range(2)]
        for e in range(2):
            m_prev, acc_prev = state[e]
            s = scores[e]
            if diagonal:
                row = lax.broadcasted_iota(jnp.int32, (t, 1), 0)
                colk = lax.broadcasted_iota(jnp.int32, (1, t), 1)
                s = jnp.where(colk <= row, s, NEG)
            m_new = jnp.maximum(m_prev, jnp.max(s, axis=-1, keepdims=True))
            p = jnp.exp2(s - m_new).astype(BF16)
            acc = jnp.exp2(m_prev - m_new) * acc_prev + _dot_nt(p, vt_ref[e, j])
            out.append((m_new, acc))
        return tuple(out)

    init = tuple((jnp.full((t, 1), NEG, F32), jnp.zeros((t, LANES), F32)) for _ in range(2))
    state = lax.fori_loop(0, qi // 2, lambda i, st: tile(2 * i + 1, tile(2 * i, st, False), False), init)
    state = lax.cond(qi % 2 == 1, lambda st: tile(qi - 1, st, False), lambda st: st, state)
    (_, a0), (_, a1) = tile(qi, state, True)
    o0 = a0[:, 0:HD_A] / a0[:, HD_A:HD_A + 1]
    o1 = a1[:, 0:HD_A] / a1[:, HD_A:HD_A + 1]
    o_ref[...] = jnp.concatenate([o0, o1], axis=-1).astype(BF16)


def _fox_prompt(q, ktb, vtb, ck):
    n_seq, _, nb, _, t = ktb.shape
    return pl.pallas_call(
        _fox_prompt_kernel,
        grid=(n_seq, N_PAIR, nb),
        in_specs=[pl.BlockSpec((t, LANES), lambda b, h, i: (b * nb + i, h)),
                  pl.BlockSpec((None, None, nb, LANES, t), lambda b, h, i: (b, h, 0, 0, 0)),
                  pl.BlockSpec((None, 2, nb, LANES, t), lambda b, h, i: (b, h, 0, 0, 0)),
                  pl.BlockSpec((None, None, nb, 2, t), lambda b, h, i: (b, h, 0, 0, 0))],
        out_specs=pl.BlockSpec((t, LANES), lambda b, h, i: (b * nb + i, h)),
        out_shape=jax.ShapeDtypeStruct(q.shape, BF16),
        compiler_params=_cparams("parallel", "parallel", "parallel"),
        name="fox_prompt",
    )(q, ktb, vtb, ck)


def _fox_decode_kernel(pt_ref, q_ref, kn_ref, vn_ref, fn_ref, ft_ref, *refs, n_pages, n_seq):
    k_refs = refs[:n_seq * n_pages]
    v_refs = refs[n_seq * n_pages:2 * n_seq * n_pages]
    o_ref = refs[2 * n_seq * n_pages]
    ps = PAGE_SIZE
    b0 = pl.program_id(0) * n_seq

    lane = lax.broadcasted_iota(jnp.int32, (H_A, D_A), 1)
    hrow = lax.broadcasted_iota(jnp.int32, (H_A, D_A), 0)
    own = (lane // HD_A) == hrow
    pos = lax.broadcasted_iota(jnp.int32, (H_A, ps), 1)

    for sq in range(n_seq):
        qbd_f = jnp.where(own, q_ref[sq].astype(F32), 0.0)
        qbd = qbd_f.astype(BF16)

        carry = fn_ref[sq]
        bias = [None] * n_pages
        for j in reversed(range(n_pages)):
            lf = ft_ref[pt_ref[b0 + sq, j]]
            x = lf
            d = 1
            while d < ps:
                x = x + jnp.where(pos + d < ps, pltpu.roll(x, ps - d, 1), 0.0)
                d *= 2
            bias[j] = (x - lf) + carry
            carry = carry + x[:, 0:1]

        kn = kn_ref[sq].astype(BF16).astype(F32)
        s_new = jnp.sum(qbd_f * kn, axis=-1, keepdims=True)
        scores = []
        m = s_new
        for j in range(n_pages):
            s = _dot(qbd, k_refs[sq * n_pages + j][...].astype(BF16)) + bias[j]
            scores.append(s)
            m = jnp.maximum(m, jnp.max(s, axis=-1, keepdims=True))
        p_new = jnp.exp(s_new - m)
        l = p_new
        vn = vn_ref[sq].astype(BF16).astype(F32)
        acc = p_new.astype(BF16).astype(F32) * vn
        for j in range(n_pages):
            p = jnp.exp(scores[j] - m)
            l = l + jnp.sum(p, axis=-1, keepdims=True)
            acc = acc + _dot_nt(p.astype(BF16), v_refs[sq * n_pages + j][...].astype(BF16))
        o = jnp.where(own, acc / l, 0.0)
        o_ref[sq] = jnp.sum(o, axis=0, keepdims=True).astype(BF16)


def _fox_decode(page_table, q, k_new, v_new, f_new, cache_kt, cache_vt, cache_ft, layer):
    bsz, n_pages = page_table.shape
    n_seq = DECODE_SEQS
    n_phys = cache_ft.shape[0]
    row = lambda dt_shape: pl.BlockSpec((n_seq,) + dt_shape, lambda b, pt: (b, 0, 0))
    kv_specs = [pl.BlockSpec((None, None, D_A, PAGE_SIZE), functools.partial(
        lambda b, pt, sq, j: (pt[b * n_seq + sq, j], layer, 0, 0), sq=sq, j=j))
        for sq in range(n_seq) for j in range(n_pages)]
    f_spec = pl.BlockSpec((n_phys, None, H_A, PAGE_SIZE), lambda b, pt: (0, layer, 0, 0),
                          pipeline_mode=pl.Buffered(1))
    grid_spec = pltpu.PrefetchScalarGridSpec(
        num_scalar_prefetch=1,
        grid=(bsz // n_seq,),
        in_specs=[row((1, D_A)), row((1, D_A)), row((1, D_A)), row((H_A, 1)), f_spec] + kv_specs + kv_specs,
        out_specs=row((1, D_A)),
    )
    n_kv = n_seq * n_pages
    return pl.pallas_call(
        functools.partial(_fox_decode_kernel, n_pages=n_pages, n_seq=n_seq),
        grid_spec=grid_spec,
        out_shape=jax.ShapeDtypeStruct((bsz, 1, D_A), BF16),
        compiler_params=_cparams("parallel"),
        name="fox_decode",
    )(page_table, q, k_new, v_new, f_new, cache_ft, *([cache_kt] * n_kv), *([cache_vt] * n_kv))


def _ab_step_out_kernel(x_ref, g_ref, ya_ref, bg_ref, cx_ref, m1_ref, m2_ref, cw_ref, wo_ref, o_ref):
    o_ref[...] = _short_conv_out(x_ref[...], ya_ref[...], bg_ref[...], cx_ref[...], m1_ref[...], m2_ref[...],
                                 cw_ref, wo_ref, g_ref[...])


def _ab_step_out(x, g, ya, bg, cx, conv_w, w_out, state):
    m = x.shape[0]
    return pl.pallas_call(
        _ab_step_out_kernel,
        grid=(1,),
        in_specs=[_row_spec(m, D_MODEL), _full_spec((1, D_MODEL)), _row_spec(m, D_A), _row_spec(m, D_B),
                  _row_spec(m, D_B), _row_spec(m, D_B), _row_spec(m, D_B), _full_spec((CONV_B, D_B)),
                  _full_spec((D_MODEL, D_MODEL))],
        out_specs=_row_spec(m, D_MODEL),
        out_shape=jax.ShapeDtypeStruct((m, D_MODEL), F32),
        compiler_params=_cparams("arbitrary"),
        name="ab_step_out",
    )(x, g, ya, bg, cx, state[1], state[0], conv_w, w_out)


def _lru_gates(xc, wgi_ref, brg_ref, big_ref, lam_ref):
    r_parts, i_parts = [], []
    for n in range(N_BLK_C):
        ri = _dot(xc[:, n * BS_C:(n + 1) * BS_C].astype(BF16), wgi_ref[n])
        r_parts.append(ri[:, :BS_C])
        i_parts.append(ri[:, BS_C:])
    r = _sigmoid(jnp.concatenate(r_parts, axis=-1) + brg_ref[...])
    ig = _sigmoid(jnp.concatenate(i_parts, axis=-1) + big_ref[...])
    log_a = -LRU_C * r * _softplus(-lam_ref[...])
    a = jnp.exp(log_a)
    th = jnp.tanh(log_a)
    u = jnp.sqrt(-2.0 * th / (1.0 - th)) * (ig * xc)
    return a, u


def _c_mix_kernel(x_ref, g2_ref, g3_ref, win_ref, cw_ref, cb_ref, wgi_ref, brg_ref, big_ref, lam_ref, wout_ref,
                  o_ref, xr_tail_ref, h_tail_ref, prev_ref, h_ref, a_ref, u_ref):
    tm = x_ref.shape[0]
    ngrp = tm // SUBLANES

    @pl.when(pl.program_id(1) == 0)
    def _():
        prev_ref[...] = jnp.zeros_like(prev_ref)
        h_ref[...] = jnp.zeros_like(h_ref)

    x = x_ref[...]
    xn = _rms(x, g2_ref[...]).astype(BF16)
    gate = _dot(xn, win_ref[:, 0:D_RNN])
    xr = _dot(xn, win_ref[:, D_RNN:])
    m1, m2, m3 = _shifted_rows(xr, prev_ref[...], CONV_C - 1)
    xc = m3 * cw_ref[0:1, :] + m2 * cw_ref[1:2, :] + m1 * cw_ref[2:3, :] + xr * cw_ref[3:4, :] + cb_ref[...]
    a, u = _lru_gates(xc, wgi_ref, brg_ref, big_ref, lam_ref)

    a = a.reshape(ngrp, SUBLANES, D_RNN)
    u = u.reshape(ngrp, SUBLANES, D_RNN)
    sub = lax.broadcasted_iota(jnp.int32, (1, SUBLANES, 1), 1)
    d = 1
    while d < SUBLANES:
        ok = sub >= d
        a_sh = jnp.where(ok, pltpu.roll(a, d, 1), 1.0)
        u_sh = jnp.where(ok, pltpu.roll(u, d, 1), 0.0)
        u = a * u_sh + u
        a = a * a_sh
        d *= 2
    a_ref[...] = a
    u_ref[...] = u

    def group(i, h):
        hs = a_ref[i] * h + u_ref[i]
        u_ref[i] = hs
        return hs[SUBLANES - 1:SUBLANES, :]

    h_ref[...] = lax.fori_loop(0, ngrp, group, h_ref[...], unroll=4)
    hs = u_ref[...].reshape(tm, D_RNN)
    z = _gelu_tanh(gate) * hs
    o_ref[...] = x + _rms(_dot(z.astype(BF16), wout_ref[...]), g3_ref[...])
    prev_ref[...] = xr[tm - SUBLANES:, :]
    xr_tail_ref[...] = xr[tm - SUBLANES:, :]
    h_tail_ref[...] = hs[tm - SUBLANES:, :]


def _c_mix(x, g2, g3, w_in, conv_w, conv_b, wgi, b_rg, b_ig, lam, w_out, n_seq, tm):
    m = x.shape[0]
    nt = m // n_seq // tm
    rows = pl.BlockSpec((tm, D_MODEL), lambda b, t: (b * nt + t, 0))
    full = lambda shape: pl.BlockSpec(shape, lambda b, t: (0,) * len(shape))
    tail = pl.BlockSpec((None, SUBLANES, D_RNN), lambda b, t: (b, 0, 0))
    tail_shape = jax.ShapeDtypeStruct((n_seq, SUBLANES, D_RNN), F32)
    return pl.pallas_call(
        _c_mix_kernel,
        grid=(n_seq, nt),
        in_specs=[rows, full((1, D_MODEL)), full((1, D_MODEL)), full((D_MODEL, 2 * D_RNN)), full((CONV_C, D_RNN)),
                  full((1, D_RNN)), full((N_BLK_C, BS_C, 2 * BS_C)), full((1, D_RNN)), full((1, D_RNN)),
                  full((1, D_RNN)), full((D_RNN, D_MODEL))],
        out_specs=[rows, tail, tail],
        out_shape=[jax.ShapeDtypeStruct((m, D_MODEL), F32), tail_shape, tail_shape],
        scratch_shapes=[pltpu.VMEM((SUBLANES, D_RNN), F32), pltpu.VMEM((1, D_RNN), F32),
                        pltpu.VMEM((tm // SUBLANES, SUBLANES, D_RNN), F32),
                        pltpu.VMEM((tm // SUBLANES, SUBLANES, D_RNN), F32)],
        compiler_params=_cparams("parallel", "arbitrary"),
        name="c_mix",
    )(x, g2, g3, w_in, conv_w, conv_b, wgi, b_rg, b_ig, lam, w_out)


def _c_step_kernel(x_ref, g2_ref, g3_ref, win_ref, s0_ref, s1_ref, s2_ref, h0_ref, cw_ref, cb_ref, wgi_ref,
                   brg_ref, big_ref, lam_ref, wout_ref, o_ref, xr_ref, h_ref):
    x = x_ref[...]
    xn = _rms(x, g2_ref[...]).astype(BF16)
    gate = _dot(xn, win_ref[:, 0:D_RNN])
    xr = _dot(xn, win_ref[:, D_RNN:])
    xc = (s0_ref[...] * cw_ref[0:1, :] + s1_ref[...] * cw_ref[1:2, :] + s2_ref[...] * cw_ref[2:3, :]
          + xr * cw_ref[3:4, :] + cb_ref[...])
    a, u = _lru_gates(xc, wgi_ref, brg_ref, big_ref, lam_ref)
    h = a * h0_ref[...] + u
    z = _gelu_tanh(gate) * h
    o_ref[...] = x + _rms(_dot(z.astype(BF16), wout_ref[...]), g3_ref[...])
    xr_ref[...] = xr
    h_ref[...] = h


def _c_step(x, g2, g3, w_in, state, h0, conv_w, conv_b, wgi, b_rg, b_ig, lam, w_out):
    m = x.shape[0]
    rows = _row_spec(m, D_RNN)
    shape = jax.ShapeDtypeStruct((m, D_RNN), F32)
    return pl.pallas_call(
        _c_step_kernel,
        grid=(1,),
        in_specs=[rows, _full_spec((1, D_MODEL)), _full_spec((1, D_MODEL)), _full_spec((D_MODEL, 2 * D_RNN)),
                  rows, rows, rows, rows, _full_spec((CONV_C, D_RNN)), _full_spec((1, D_RNN)),
                  _full_spec((N_BLK_C, BS_C, 2 * BS_C)), _full_spec((1, D_RNN)), _full_spec((1, D_RNN)),
                  _full_spec((1, D_RNN)), _full_spec((D_RNN, D_MODEL))],
        out_specs=[rows, rows, rows],
        out_shape=[shape, shape, shape],
        compiler_params=_cparams("arbitrary"),
        name="c_step",
    )(x, g2, g3, w_in, state[0], state[1], state[2], h0, conv_w, conv_b, wgi, b_rg, b_ig, lam, w_out)


def kernel(x_prompt, x_sample, p_prompt, p_sample, cache_k, cache_v, cache_logf, state_conv_b, state_conv_c, state_h_c, page_table, norms, ffn1_wi, ffn1_wo, ffn2_wi, ffn2_wo, ple_w_pe, ple_w_pg, ab_w_in, ab_b_f, ab_conv_w, ab_w_out, c_w_in, c_conv_w, c_conv_b, c_w_rg, c_b_rg, c_w_ig, c_b_ig, c_lam, c_w_out):
    bp, s_len, _ = x_prompt.shape
    bs = x_sample.shape[0]
    depth = norms.shape[0]
    n_phys, n_ab = cache_k.shape[0], cache_k.shape[1]
    mp = bp * s_len
    tm_x = min(512, s_len)

    xp = x_prompt.reshape(mp, D_MODEL)
    xs = x_sample.reshape(bs, D_MODEL)
    pp = p_prompt.reshape(depth, mp, D_PLE)
    psm = p_sample.reshape(depth, bs, D_PLE)
    cache_kt = jnp.transpose(cache_k, (0, 1, 3, 4, 2)).reshape(n_phys, n_ab, D_A, PAGE_SIZE)
    cache_vt = jnp.transpose(cache_v, (0, 1, 3, 4, 2)).reshape(n_phys, n_ab, D_A, PAGE_SIZE)
    cache_ft = jnp.swapaxes(cache_logf, 2, 3)
    w_pe = ple_w_pe.astype(BF16)
    w_pg = ple_w_pg.astype(BF16)
    wi1, wo1 = ffn1_wi.astype(BF16), ffn1_wo.astype(BF16)
    wi2, wo2 = ffn2_wi.astype(BF16), ffn2_wo.astype(BF16)

    kp_l, vp_l, fp_l, ks_l, vs_l, fs_l = [], [], [], [], [], []
    cbp_l, cbs_l, ccp_l, ccs_l, hp_l, hs_l = [], [], [], [], [], []
    for i in range(depth):
        g = norms[i].reshape(norms.shape[1], 1, D_MODEL)
        xp = _ffn_half(xp, g[0], g[1], wi1, wo1, i, tm_x)
        xs = _ffn_half(xs, g[0], g[1], wi1, wo1, i, bs)
        l = i // 2
        if i % 2 == 0:
            w_in = ab_w_in[l]
            wa = jnp.concatenate([w_in[:, :D_A], w_in[:, 3 * D_A + H_A:]], axis=1).astype(BF16)
            wkvf = jnp.pad(w_in[:, D_A:3 * D_A + H_A].T, ((0, BF16_ROWS - H_A), (0, 0))).astype(BF16)
            b_f = ab_b_f[l].reshape(H_A, 1)
            w_out = ab_w_out[l].astype(BF16)
            q, kt, vt, ktb, vtb, lft, bg, cx = _ab_proj(xp, g[2], wa, wkvf, b_f, bp, tm_x, HD_A ** -0.5 * LOG2E)
            nb = s_len // tm_x
            ck = _cumsum_lanes(lft).reshape(bp, N_PAIR, 2, nb, tm_x).transpose(0, 1, 3, 2, 4)
            ya = _fox_prompt(q, ktb, vtb, ck)
            mix = (g[3], ya, bg, cx, ab_conv_w[l], w_out, s_len)
            kp_l.append(kt)
            vp_l.append(vt)
            fp_l.append(lft)
            cbp_l.append(cx.reshape(bp, s_len, D_B)[:, s_len - (CONV_B - 1):])
            q, kt, vt, _, _, lft, bg, cx = _ab_proj(xs, g[2], wa, wkvf, b_f, 1, bs, HD_A ** -0.5)
            ya = _fox_decode(page_table, q.reshape(bs, 1, D_A), kt[0].T.reshape(bs, 1, D_A),
                             vt[0].T.reshape(bs, 1, D_A), lft[0].T.reshape(bs, H_A, 1),
                             cache_kt, cache_vt, cache_ft, l)
            st = state_conv_b[:, l]
            xs = _ab_step_out(xs, g[3], ya.reshape(bs, D_A), bg, cx, ab_conv_w[l], w_out, (st[:, 0], st[:, 1]))
            ks_l.append(kt[0])
            vs_l.append(vt[0])
            fs_l.append(lft[0])
            cbs_l.append(jnp.stack([st[:, 1], cx], axis=1))
        else:
            w_in = c_w_in[l].astype(BF16)
            wgi = jnp.concatenate([c_w_rg[l], c_w_ig[l]], axis=-1).astype(BF16)
            row = lambda a: a.reshape(1, D_RNN)
            gate_args = (c_conv_w[l], row(c_conv_b[l]), wgi, row(c_b_rg[l]), row(c_b_ig[l]), row(c_lam[l]),
                         c_w_out[l].astype(BF16))
            xp, xr_tail, h_tail = _c_mix(xp, g[2], g[3], w_in, *gate_args, bp, tm_x)
            mix = None
            ccp_l.append(xr_tail[:, SUBLANES - (CONV_C - 1):])
            hp_l.append(h_tail[:, SUBLANES - 1])
            st = state_conv_c[:, l]
            xs, xr, h = _c_step(xs, g[2], g[3], w_in, (st[:, 0], st[:, 1], st[:, 2]), state_h_c[:, l], *gate_args)
            ccs_l.append(jnp.stack([st[:, 1], st[:, 2], xr], axis=1))
            hs_l.append(h)
        xp = _ffn_half(xp, g[4], g[5], wi2, wo2, i, tm_x, ple=(pp, g[6], w_pe, w_pg), mix=mix)
        xs = _ffn_half(xs, g[4], g[5], wi2, wo2, i, bs, ple=(psm, g[6], w_pe, w_pg))

    st = lambda lst: jnp.stack(lst, axis=1)
    kv_p = lambda lst: st(lst).reshape(bp, n_ab, H_A, HD_A, s_len).transpose(0, 1, 4, 2, 3)
    kv_s = lambda lst: jnp.stack(lst, axis=0).reshape(n_ab, 1, H_A, HD_A, bs).transpose(4, 0, 1, 2, 3)
    return (xp.reshape(bp, s_len, D_MODEL), xs.reshape(bs, 1, D_MODEL),
            kv_p(kp_l), kv_p(vp_l), st(fp_l).transpose(0, 1, 3, 2),
            kv_s(ks_l), kv_s(vs_l), jnp.stack(fs_l, axis=0).reshape(n_ab, 1, H_A, bs).transpose(3, 0, 1, 2),
            st(cbp_l), st(cbs_l), st(ccp_l), st(ccs_l), st(hp_l), st(hs_l))
```

```python
import functools

import jax
import jax.numpy as jnp
from jax import lax
from jax.experimental import pallas as pl
from jax.experimental.pallas import tpu as pltpu

F32 = jnp.float32
BF16 = jnp.bfloat16

D_MODEL = 1024
D_PLE = 256
H_A = 8
HD_A = 64
D_A = H_A * HD_A
D_B = D_MODEL - D_A
CONV_B = 3
D_RNN = D_MODEL
N_BLK_C = 8
BS_C = D_RNN // N_BLK_C
CONV_C = 4
LRU_C = 8.0
D_FF = 2816
EPS = 1e-6
NEG = -1e30
PAGE_SIZE = 128

LANES = 128
SUBLANES = 8
BF16_ROWS = 16
FF_CHUNKS = ((0, 1024), (1024, 1024), (2048, 768))
FF_STEP = 256
LOG2E = 1.4426950408889634
DECODE_SEQS = 2
N_PAIR = D_A // LANES
KVF_ROWS = 2 * D_A + BF16_ROWS
VMEM_LIMIT = 56 * 1024 * 1024

_NT = (((1,), (1,)), ((), ()))


def _cparams(*sem):
    return pltpu.CompilerParams(dimension_semantics=sem, vmem_limit_bytes=VMEM_LIMIT)


def _rms(x, g):
    return x * lax.rsqrt(jnp.mean(x * x, axis=-1, keepdims=True) + EPS) * g


def _dot(a, b):
    return jnp.dot(a, b, preferred_element_type=F32)


def _dot_nt(a, b):
    return lax.dot_general(a, b, _NT, preferred_element_type=F32)


def _log_sigmoid(x):
    return jnp.minimum(x, 0.0) - jnp.log1p(jnp.exp(-jnp.abs(x)))


def _softplus(x):
    return jnp.maximum(x, 0.0) + jnp.log1p(jnp.exp(-jnp.abs(x)))


def _sigmoid(x):
    return 1.0 / (1.0 + jnp.exp(-x))


def _gelu_tanh(x):
    c = 0.7978845608028654
    return 0.5 * x * (1.0 + jnp.tanh(c * (x + 0.044715 * (x * x * x))))


def _row_spec(tm, d):
    return pl.BlockSpec((tm, d), lambda i: (i, 0))


def _full_spec(shape):
    nd = len(shape)
    return pl.BlockSpec(shape, lambda i: (0,) * nd)


def _shifted_rows(cur, prev, n_shift):
    sub = lax.broadcasted_iota(jnp.int32, (SUBLANES, 1), 0)
    out = []
    for sft in range(1, n_shift + 1):
        r = pltpu.roll(cur, sft, 0)
        top = r[0:SUBLANES, :]
        for i in range(sft):
            top = jnp.where(sub == i, prev[SUBLANES - sft + i:SUBLANES - sft + i + 1, :], top)
        out.append(jnp.concatenate([top, r[SUBLANES:, :]], axis=0))
    return out


def _short_conv_out(x, ya, bg, cx, m1, m2, cw_ref, wo_ref, g):
    conv = m2 * cw_ref[0:1, :] + m1 * cw_ref[1:2, :] + cx * cw_ref[2:3, :]
    yb = bg * conv
    y = _dot(ya, wo_ref[0:D_A, :]) + _dot(yb.astype(BF16), wo_ref[D_A:, :])
    return x + _rms(y, g)


def _ffn_kernel(*refs, with_mix, with_ple, tiles_per_seq):
    refs = list(refs)
    x = refs.pop(0)[...]
    if with_mix:
        gmix_ref, ya_ref, bg_ref, cx_ref, prev_ref, cw_ref, wmix_ref = refs[:7]
        del refs[:7]
        cx = cx_ref[...]
        first = (pl.program_id(0) % tiles_per_seq) == 0
        m1, m2 = _shifted_rows(cx, prev_ref[...] * jnp.where(first, 0.0, 1.0), CONV_B - 1)
        x = _short_conv_out(x, ya_ref[...], bg_ref[...], cx, m1, m2, cw_ref, wmix_ref, gmix_ref[...])
    gpre_ref, gpost_ref, wig_ref, wiu_ref, wo_ref = refs[:5]
    del refs[:5]
    xn = _rms(x, gpre_ref[...]).astype(BF16)
    acc = None
    for c0, cw in FF_CHUNKS:
        g = _dot(xn, wig_ref[:, c0:c0 + cw])
        u = _dot(xn, wiu_ref[:, c0:c0 + cw])
        h = (g * _sigmoid(g)) * u
        part = _dot(h.astype(BF16), wo_ref[c0:c0 + cw, :])
        acc = part if acc is None else acc + part
    x = x + 0.5 * _rms(acc, gpost_ref[...])
    if with_ple:
        x = _ple(x, *refs[:4])
        del refs[:4]
    o_ref, = refs
    o_ref[...] = x


def _ple(x, p_ref, g_ref, wpe_ref, wpg_ref):
    gate = _sigmoid(_dot(_rms(x, g_ref[...]).astype(BF16), wpg_ref[...]))
    return x + _dot(p_ref[...].astype(BF16), wpe_ref[...]) * gate


def _ple_specs(layer, tm, index):
    resident = lambda shape: pl.BlockSpec((None,) + shape, lambda *_: (layer, 0, 0), pipeline_mode=pl.Buffered(1))
    return [pl.BlockSpec((None, tm, D_PLE), lambda *i: (layer,) + index(*i)),
            pl.BlockSpec((1, D_MODEL), lambda *_: (0, 0)), resident((D_PLE, D_MODEL)), resident((D_MODEL, D_MODEL))]


def _ffn_half(x, g_pre, g_post, w, layer, tm, ple=None, mix=None):
    m = x.shape[0]
    resident = lambda shape: pl.BlockSpec(shape, lambda i: (0, 0), pipeline_mode=pl.Buffered(1))
    in_specs = [_row_spec(tm, D_MODEL)]
    args = [x]
    tiles_per_seq = 1
    if mix is not None:
        g_mix, ya, bg, cx, conv_w, w_out, seq_len = mix
        tiles_per_seq = seq_len // tm
        per = tm // SUBLANES
        in_specs += [_full_spec((1, D_MODEL)), _row_spec(tm, D_A), _row_spec(tm, D_B), _row_spec(tm, D_B),
                     pl.BlockSpec((SUBLANES, D_B), lambda i: (jnp.maximum(i * per - 1, 0), 0)),
                     _full_spec((CONV_B, D_B)), _full_spec((D_MODEL, D_MODEL))]
        args += [g_mix, ya, bg, cx, cx, conv_w, w_out]
    in_specs += [_full_spec((1, D_MODEL)), _full_spec((1, D_MODEL)),
                 resident((D_MODEL, D_FF)), resident((D_MODEL, D_FF)), resident((D_FF, D_MODEL))]
    args += [g_pre, g_post, *w]
    if ple is not None:
        in_specs += _ple_specs(layer, tm, lambda i: (i, 0))
        args += list(ple)
    return pl.pallas_call(
        functools.partial(_ffn_kernel, with_mix=mix is not None, with_ple=ple is not None,
                          tiles_per_seq=tiles_per_seq),
        grid=(m // tm,),
        in_specs=in_specs,
        out_specs=_row_spec(tm, D_MODEL),
        out_shape=jax.ShapeDtypeStruct((m, D_MODEL), F32),
        compiler_params=_cparams("parallel"),
        name="ffn_half",
    )(*args)


def _ffn_step_kernel(*refs, with_ple):
    refs = list(refs)
    x_ref, gpre_ref, gpost_ref, wig_ref, wiu_ref, wo_ref = refs[:6]
    ple_refs = refs[6:10] if with_ple else []
    o_ref, wigb_ref, wiub_ref, wob_ref, xn_ref, acc_ref = refs[-6:]
    j = pl.program_id(0)

    @pl.when(j == 0)
    def _():
        xn_ref[...] = _rms(x_ref[...], gpre_ref[...]).astype(BF16)
        acc_ref[...] = jnp.zeros_like(acc_ref)

    wig = wig_ref[...].astype(BF16)
    wiu = wiu_ref[...].astype(BF16)
    wo = wo_ref[...].astype(BF16)
    wigb_ref[...] = wig
    wiub_ref[...] = wiu
    wob_ref[...] = wo
    xn = xn_ref[...]
    g = _dot(xn, wig)
    u = _dot(xn, wiu)
    acc_ref[...] += _dot(((g * _sigmoid(g)) * u).astype(BF16), wo)

    @pl.when(j == pl.num_programs(0) - 1)
    def _():
        x = x_ref[...] + 0.5 * _rms(acc_ref[...], gpost_ref[...])
        if with_ple:
            x = _ple(x, *ple_refs)
        o_ref[...] = x


def _ffn_step(x, g_pre, g_post, wi, wo, layer, ple=None):
    m = x.shape[0]
    nf = D_FF // FF_STEP
    const = lambda shape: pl.BlockSpec(shape, lambda j: (0,) * len(shape))
    in_specs = [const((m, D_MODEL)), const((1, D_MODEL)), const((1, D_MODEL)),
                pl.BlockSpec((None, D_MODEL, FF_STEP), lambda j: (layer, 0, j)),
                pl.BlockSpec((None, D_MODEL, FF_STEP), lambda j: (layer, 0, j + nf)),
                pl.BlockSpec((None, FF_STEP, D_MODEL), lambda j: (layer, j, 0))]
    args = [x, g_pre, g_post, wi, wi, wo]
    if ple is not None:
        in_specs += _ple_specs(layer, m, lambda j: (0, 0))
        args += list(ple)
    up_spec = pl.BlockSpec((D_MODEL, FF_STEP), lambda j: (0, j))
    up_shape = jax.ShapeDtypeStruct((D_MODEL, D_FF), BF16)
    out = pl.pallas_call(
        functools.partial(_ffn_step_kernel, with_ple=ple is not None),
        grid=(nf,),
        in_specs=in_specs,
        out_specs=[const((m, D_MODEL)), up_spec, up_spec, pl.BlockSpec((FF_STEP, D_MODEL), lambda j: (j, 0))],
        out_shape=[jax.ShapeDtypeStruct((m, D_MODEL), F32), up_shape, up_shape,
                   jax.ShapeDtypeStruct((D_FF, D_MODEL), BF16)],
        scratch_shapes=[pltpu.VMEM((m, D_MODEL), BF16), pltpu.VMEM((m, D_MODEL), F32)],
        compiler_params=_cparams("arbitrary"),
        name="ffn_step",
    )(*args)
    return out[0], tuple(out[1:])


def _ab_proj_kernel(*refs, q_scale):
    x_ref, g_ref, wa_ref, wkvf_ref, bf_ref = refs[:5]
    q_ref, kt_ref, vt_ref, ktb_ref, vtb_ref, lft_ref, bg_ref, cx_ref = refs[-8:]
    tm = x_ref.shape[0]
    xn = _rms(x_ref[...], g_ref[...]).astype(BF16)
    col = lambda c: _dot(xn, wa_ref[:, c * D_A:(c + 1) * D_A])
    q_ref[...] = (col(0) * q_scale).astype(BF16)
    bg_ref[...] = col(1)
    cx_ref[...] = col(2) * col(3)
    kvf = _dot_nt(wkvf_ref[...], xn)
    kt = kvf[0:D_A]
    vt = kvf[D_A:2 * D_A]
    kt_ref[...] = kt
    vt_ref[...] = vt
    ktb_ref[...] = kt.astype(BF16).reshape(N_PAIR, LANES, tm)
    vtb_ref[:, 0:HD_A, :] = vt.astype(BF16).reshape(H_A, HD_A, tm)
    ones_row = lax.broadcasted_iota(jnp.int32, (H_A, LANES - HD_A, tm), 1) == 0
    vtb_ref[:, HD_A:, :] = jnp.where(ones_row, 1.0, 0.0).astype(BF16)
    lft_ref[...] = _log_sigmoid(kvf[2 * D_A:2 * D_A + H_A] + bf_ref[...])


def _ab_proj(x, g, wa, wkvf, b_f, n_seq, tm, q_scale, layer, n_layers, kv_prev=None):
    m = x.shape[0]
    s_len = m // n_seq
    nt = s_len // tm
    rows = lambda d: pl.BlockSpec((tm, d), lambda b, t: (b * nt + t, 0))
    full = lambda shape: pl.BlockSpec(shape, lambda b, t: (0,) * len(shape))
    kv_spec = pl.BlockSpec((None, None, D_A, tm), lambda b, t: (b, layer, 0, t))
    kv_shape = jax.ShapeDtypeStruct((n_seq, n_layers, D_A, s_len), F32)
    tile_spec = lambda n: pl.BlockSpec((None, n, None, LANES, tm), lambda b, t: (b, 0, t, 0, 0))
    wide = lambda dt: jax.ShapeDtypeStruct((m, D_A), dt)
    tile_shape = lambda n: jax.ShapeDtypeStruct((n_seq, n, nt, LANES, tm), BF16)
    in_specs = [rows(D_MODEL), full((1, D_MODEL)), full((D_MODEL, 4 * D_A)), full((KVF_ROWS, D_MODEL)),
                full((H_A, 1))]
    args = [x, g, wa, wkvf, b_f]
    aliases = {}
    if kv_prev is not None:
        in_specs += [pl.BlockSpec(memory_space=pl.ANY)] * 2
        args += list(kv_prev)
        aliases = {5: 1, 6: 2}
    return pl.pallas_call(
        functools.partial(_ab_proj_kernel, q_scale=q_scale),
        grid=(n_seq, nt),
        in_specs=in_specs,
        out_specs=[rows(D_A), kv_spec, kv_spec, tile_spec(N_PAIR), tile_spec(H_A),
                   pl.BlockSpec((None, H_A, tm), lambda b, t: (b, 0, t)), rows(D_B), rows(D_B)],
        out_shape=[wide(BF16), kv_shape, kv_shape, tile_shape(N_PAIR), tile_shape(H_A),
                   jax.ShapeDtypeStruct((n_seq, H_A, s_len), F32), wide(F32), wide(F32)],
        input_output_aliases=aliases,
        compiler_params=_cparams("parallel", "parallel"),
        name="ab_proj",
    )(*args)


def _cumsum_kernel(x_ref, o_ref):
    x = x_ref[...]
    n = x.shape[1]
    pos = lax.broadcasted_iota(jnp.int32, (1, n), 1)
    d = 1
    while d < n:
        x = x + jnp.where(pos >= d, pltpu.roll(x, d, 1), 0.0)
        d *= 2
    o_ref[...] = x * LOG2E


def _cumsum_lanes(x):
    spec = pl.BlockSpec((None,) + x.shape[1:], lambda b: (b, 0, 0))
    return pl.pallas_call(
        _cumsum_kernel,
        grid=(x.shape[0],),
        in_specs=[spec],
        out_specs=spec,
        out_shape=jax.ShapeDtypeStruct(x.shape, F32),
        compiler_params=_cparams("parallel"),
        name="logf_cumsum",
    )(x)


def _fox_prompt_kernel(q_ref, kt_ref, vt_ref, ck_ref, o_ref):
    qi = pl.program_id(2)
    t = q_ref.shape[0]
    lane = lax.broadcasted_iota(jnp.int32, (1, LANES), 1)
    q = q_ref[...]
    zero = jnp.zeros_like(q)
    qs = (jnp.where(lane < HD_A, q, zero), jnp.where(lane >= HD_A, q, zero))

    def tile(j, state, diagonal):
        kt = kt_ref[j]
        ck = ck_ref[j]
        out = []
        scores = [_dot(qs[e], kt) - ck[e:e + 1, :] for e in range(2)]
        for e in range(2):
            m_prev, acc_prev = state[e]
            s = scores[e]
            if diagonal:
                row = lax.broadcasted_iota(jnp.int32, (t, 1), 0)
                colk = lax.broadcasted_iota(jnp.int32, (1, t), 1)
                s = jnp.where(colk <= row, s, NEG)
            m_new = jnp.maximum(m_prev, jnp.max(s, axis=-1, keepdims=True))
            p = jnp.exp2(s - m_new).astype(BF16)
            acc = jnp.exp2(m_prev - m_new) * acc_prev + _dot_nt(p, vt_ref[e, j])
            out.append((m_new, acc))
        return tuple(out)

    init = tuple((jnp.full((t, 1), NEG, F32), jnp.zeros((t, LANES), F32)) for _ in range(2))
    state = lax.fori_loop(0, qi // 2, lambda i, st: tile(2 * i + 1, tile(2 * i, st, False), False), init)
    (_, a0), (_, a1) = lax.cond(qi % 2 == 1, lambda st: tile(qi, tile(qi - 1, st, False), True),
                                lambda st: tile(qi, st, True), state)
    o0 = a0[:, 0:HD_A] / a0[:, HD_A:HD_A + 1]
    o1 = a1[:, 0:HD_A] / a1[:, HD_A:HD_A + 1]
    o_ref[...] = jnp.concatenate([o0, o1], axis=-1).astype(BF16)


def _fox_prompt(q, ktb, vtb, ck):
    n_seq, _, nb, _, t = ktb.shape
    return pl.pallas_call(
        _fox_prompt_kernel,
        grid=(n_seq, N_PAIR, nb),
        in_specs=[pl.BlockSpec((t, LANES), lambda b, h, i: (b * nb + i, h)),
                  pl.BlockSpec((None, None, nb, LANES, t), lambda b, h, i: (b, h, 0, 0, 0)),
                  pl.BlockSpec((None, 2, nb, LANES, t), lambda b, h, i: (b, h, 0, 0, 0)),
                  pl.BlockSpec((None, None, nb, 2, t), lambda b, h, i: (b, h, 0, 0, 0))],
        out_specs=pl.BlockSpec((t, LANES), lambda b, h, i: (b * nb + i, h)),
        out_shape=jax.ShapeDtypeStruct(q.shape, BF16),
        compiler_params=_cparams("parallel", "parallel", "parallel"),
        name="fox_prompt",
    )(q, ktb, vtb, ck)


def _fox_decode_kernel(pt_ref, q_ref, kn_ref, vn_ref, fn_ref, ft_ref, *refs, n_pages, n_seq):
    k_refs = refs[:n_seq * n_pages]
    v_refs = refs[n_seq * n_pages:2 * n_seq * n_pages]
    o_ref = refs[2 * n_seq * n_pages]
    ps = PAGE_SIZE
    b0 = pl.program_id(0) * n_seq

    lane = lax.broadcasted_iota(jnp.int32, (H_A, D_A), 1)
    hrow = lax.broadcasted_iota(jnp.int32, (H_A, D_A), 0)
    own = (lane // HD_A) == hrow
    pos = lax.broadcasted_iota(jnp.int32, (H_A, ps), 1)

    for sq in range(n_seq):
        qbd_f = jnp.where(own, q_ref[sq].astype(F32), 0.0)
        qbd = qbd_f.astype(BF16)

        carry = fn_ref[sq]
        bias = [None] * n_pages
        for j in reversed(range(n_pages)):
            lf = ft_ref[pt_ref[b0 + sq, j]]
            x = lf
            d = 1
            while d < ps:
                x = x + jnp.where(pos + d < ps, pltpu.roll(x, ps - d, 1), 0.0)
                d *= 2
            bias[j] = (x - lf) + carry
            carry = carry + x[:, 0:1]

        kn = kn_ref[sq].astype(BF16).astype(F32)
        s_new = jnp.sum(qbd_f * kn, axis=-1, keepdims=True)
        scores = []
        m = s_new
        for j in range(n_pages):
            s = _dot(qbd, k_refs[sq * n_pages + j][...].astype(BF16)) + bias[j]
            scores.append(s)
            m = jnp.maximum(m, jnp.max(s, axis=-1, keepdims=True))
        p_new = jnp.exp(s_new - m)
        l = p_new
        vn = vn_ref[sq].astype(BF16).astype(F32)
        acc = p_new.astype(BF16).astype(F32) * vn
        for j in range(n_pages):
            p = jnp.exp(scores[j] - m)
            l = l + jnp.sum(p, axis=-1, keepdims=True)
            acc = acc + _dot_nt(p.astype(BF16), v_refs[sq * n_pages + j][...].astype(BF16))
        o = jnp.where(own, acc / l, 0.0)
        o_ref[sq] = jnp.sum(o, axis=0, keepdims=True).astype(BF16)


def _fox_decode(page_table, q, k_new, v_new, f_new, cache_kt, cache_vt, cache_ft, layer):
    bsz, n_pages = page_table.shape
    n_seq = DECODE_SEQS
    n_phys = cache_ft.shape[0]
    row = lambda dt_shape: pl.BlockSpec((n_seq,) + dt_shape, lambda b, pt: (b, 0, 0))
    kv_specs = [pl.BlockSpec((None, None, D_A, PAGE_SIZE), functools.partial(
        lambda b, pt, sq, j: (pt[b * n_seq + sq, j], layer, 0, 0), sq=sq, j=j))
        for sq in range(n_seq) for j in range(n_pages)]
    f_spec = pl.BlockSpec((n_phys, None, H_A, PAGE_SIZE), lambda b, pt: (0, layer, 0, 0),
                          pipeline_mode=pl.Buffered(1))
    grid_spec = pltpu.PrefetchScalarGridSpec(
        num_scalar_prefetch=1,
        grid=(bsz // n_seq,),
        in_specs=[row((1, D_A)), row((1, D_A)), row((1, D_A)), row((H_A, 1)), f_spec] + kv_specs + kv_specs,
        out_specs=row((1, D_A)),
    )
    n_kv = n_seq * n_pages
    return pl.pallas_call(
        functools.partial(_fox_decode_kernel, n_pages=n_pages, n_seq=n_seq),
        grid_spec=grid_spec,
        out_shape=jax.ShapeDtypeStruct((bsz, 1, D_A), BF16),
        compiler_params=_cparams("parallel"),
        name="fox_decode",
    )(page_table, q, k_new, v_new, f_new, cache_ft, *([cache_kt] * n_kv), *([cache_vt] * n_kv))


def _ab_step_out_kernel(x_ref, g_ref, ya_ref, bg_ref, cx_ref, m1_ref, m2_ref, cw_ref, wo_ref, o_ref):
    o_ref[...] = _short_conv_out(x_ref[...], ya_ref[...], bg_ref[...], cx_ref[...], m1_ref[...], m2_ref[...],
                                 cw_ref, wo_ref, g_ref[...])


def _ab_step_out(x, g, ya, bg, cx, conv_w, w_out, state):
    m = x.shape[0]
    return pl.pallas_call(
        _ab_step_out_kernel,
        grid=(1,),
        in_specs=[_row_spec(m, D_MODEL), _full_spec((1, D_MODEL)), _row_spec(m, D_A), _row_spec(m, D_B),
                  _row_spec(m, D_B), _row_spec(m, D_B), _row_spec(m, D_B), _full_spec((CONV_B, D_B)),
                  _full_spec((D_MODEL, D_MODEL))],
        out_specs=_row_spec(m, D_MODEL),
        out_shape=jax.ShapeDtypeStruct((m, D_MODEL), F32),
        compiler_params=_cparams("arbitrary"),
        name="ab_step_out",
    )(x, g, ya, bg, cx, state[1], state[0], conv_w, w_out)


def _lru_gates(xc, wgi_ref, brg_ref, big_ref, lam_ref):
    r_parts, i_parts = [], []
    for n in range(N_BLK_C):
        ri = _dot(xc[:, n * BS_C:(n + 1) * BS_C].astype(BF16), wgi_ref[n])
        r_parts.append(ri[:, :BS_C])
        i_parts.append(ri[:, BS_C:])
    r = _sigmoid(jnp.concatenate(r_parts, axis=-1) + brg_ref[...])
    ig = _sigmoid(jnp.concatenate(i_parts, axis=-1) + big_ref[...])
    log_a = -LRU_C * r * _softplus(-lam_ref[...])
    a = jnp.exp(log_a)
    th = jnp.tanh(log_a)
    u = jnp.sqrt(-2.0 * th / (1.0 - th)) * (ig * xc)
    return a, u


def _c_mix_kernel(x_ref, g2_ref, g3_ref, win_ref, cw_ref, cb_ref, wgi_ref, brg_ref, big_ref, lam_ref, wout_ref,
                  o_ref, xr_tail_ref, h_tail_ref, prev_ref, h_ref, a_ref, u_ref):
    tm = x_ref.shape[0]
    ngrp = tm // SUBLANES

    @pl.when(pl.program_id(1) == 0)
    def _():
        prev_ref[...] = jnp.zeros_like(prev_ref)
        h_ref[...] = jnp.zeros_like(h_ref)

    x = x_ref[...]
    xn = _rms(x, g2_ref[...]).astype(BF16)
    gate = _dot(xn, win_ref[:, 0:D_RNN])
    xr = _dot(xn, win_ref[:, D_RNN:])
    m1, m2, m3 = _shifted_rows(xr, prev_ref[...], CONV_C - 1)
    xc = m3 * cw_ref[0:1, :] + m2 * cw_ref[1:2, :] + m1 * cw_ref[2:3, :] + xr * cw_ref[3:4, :] + cb_ref[...]
    a, u = _lru_gates(xc, wgi_ref, brg_ref, big_ref, lam_ref)

    a = a.reshape(ngrp, SUBLANES, D_RNN)
    u = u.reshape(ngrp, SUBLANES, D_RNN)
    sub = lax.broadcasted_iota(jnp.int32, (1, SUBLANES, 1), 1)
    d = 1
    while d < SUBLANES:
        ok = sub >= d
        a_sh = jnp.where(ok, pltpu.roll(a, d, 1), 1.0)
        u_sh = jnp.where(ok, pltpu.roll(u, d, 1), 0.0)
        u = a * u_sh + u
        a = a * a_sh
        d *= 2
    a_ref[...] = a
    u_ref[...] = u

    def group(i, h):
        hs = a_ref[i] * h + u_ref[i]
        u_ref[i] = hs
        return hs[SUBLANES - 1:SUBLANES, :]

    h_ref[...] = lax.fori_loop(0, ngrp, group, h_ref[...], unroll=4)
    hs = u_ref[...].reshape(tm, D_RNN)
    z = _gelu_tanh(gate) * hs
    o_ref[...] = x + _rms(_dot(z.astype(BF16), wout_ref[...]), g3_ref[...])
    prev_ref[...] = xr[tm - SUBLANES:, :]
    xr_tail_ref[...] = xr[tm - SUBLANES:, :]
    h_tail_ref[...] = hs[tm - SUBLANES:, :]


def _c_mix(x, g2, g3, w_in, conv_w, conv_b, wgi, b_rg, b_ig, lam, w_out, n_seq, tm):
    m = x.shape[0]
    nt = m // n_seq // tm
    rows = pl.BlockSpec((tm, D_MODEL), lambda b, t: (b * nt + t, 0))
    full = lambda shape: pl.BlockSpec(shape, lambda b, t: (0,) * len(shape))
    tail = pl.BlockSpec((None, SUBLANES, D_RNN), lambda b, t: (b, 0, 0))
    tail_shape = jax.ShapeDtypeStruct((n_seq, SUBLANES, D_RNN), F32)
    return pl.pallas_call(
        _c_mix_kernel,
        grid=(n_seq, nt),
        in_specs=[rows, full((1, D_MODEL)), full((1, D_MODEL)), full((D_MODEL, 2 * D_RNN)), full((CONV_C, D_RNN)),
                  full((1, D_RNN)), full((N_BLK_C, BS_C, 2 * BS_C)), full((1, D_RNN)), full((1, D_RNN)),
                  full((1, D_RNN)), full((D_RNN, D_MODEL))],
        out_specs=[rows, tail, tail],
        out_shape=[jax.ShapeDtypeStruct((m, D_MODEL), F32), tail_shape, tail_shape],
        scratch_shapes=[pltpu.VMEM((SUBLANES, D_RNN), F32), pltpu.VMEM((1, D_RNN), F32),
                        pltpu.VMEM((tm // SUBLANES, SUBLANES, D_RNN), F32),
                        pltpu.VMEM((tm // SUBLANES, SUBLANES, D_RNN), F32)],
        compiler_params=_cparams("parallel", "arbitrary"),
        name="c_mix",
    )(x, g2, g3, w_in, conv_w, conv_b, wgi, b_rg, b_ig, lam, w_out)


def _c_step_kernel(x_ref, g2_ref, g3_ref, win_ref, s0_ref, s1_ref, s2_ref, h0_ref, cw_ref, cb_ref, wgi_ref,
                   brg_ref, big_ref, lam_ref, wout_ref, o_ref, xr_ref, h_ref):
    x = x_ref[...]
    xn = _rms(x, g2_ref[...]).astype(BF16)
    gate = _dot(xn, win_ref[:, 0:D_RNN])
    xr = _dot(xn, win_ref[:, D_RNN:])
    xc = (s0_ref[...] * cw_ref[0:1, :] + s1_ref[...] * cw_ref[1:2, :] + s2_ref[...] * cw_ref[2:3, :]
          + xr * cw_ref[3:4, :] + cb_ref[...])
    a, u = _lru_gates(xc, wgi_ref, brg_ref, big_ref, lam_ref)
    h = a * h0_ref[...] + u
    z = _gelu_tanh(gate) * h
    o_ref[...] = x + _rms(_dot(z.astype(BF16), wout_ref[...]), g3_ref[...])
    xr_ref[...] = xr
    h_ref[...] = h


def _c_step(x, g2, g3, w_in, state, h0, conv_w, conv_b, wgi, b_rg, b_ig, lam, w_out):
    m = x.shape[0]
    rows = _row_spec(m, D_RNN)
    shape = jax.ShapeDtypeStruct((m, D_RNN), F32)
    return pl.pallas_call(
        _c_step_kernel,
        grid=(1,),
        in_specs=[rows, _full_spec((1, D_MODEL)), _full_spec((1, D_MODEL)), _full_spec((D_MODEL, 2 * D_RNN)),
                  rows, rows, rows, rows, _full_spec((CONV_C, D_RNN)), _full_spec((1, D_RNN)),
                  _full_spec((N_BLK_C, BS_C, 2 * BS_C)), _full_spec((1, D_RNN)), _full_spec((1, D_RNN)),
                  _full_spec((1, D_RNN)), _full_spec((D_RNN, D_MODEL))],
        out_specs=[rows, rows, rows],
        out_shape=[shape, shape, shape],
        compiler_params=_cparams("arbitrary"),
        name="c_step",
    )(x, g2, g3, w_in, state[0], state[1], state[2], h0, conv_w, conv_b, wgi, b_rg, b_ig, lam, w_out)


def kernel(x_prompt, x_sample, p_prompt, p_sample, cache_k, cache_v, cache_logf, state_conv_b, state_conv_c, state_h_c, page_table, norms, ffn1_wi, ffn1_wo, ffn2_wi, ffn2_wo, ple_w_pe, ple_w_pg, ab_w_in, ab_b_f, ab_conv_w, ab_w_out, c_w_in, c_conv_w, c_conv_b, c_w_rg, c_b_rg, c_w_ig, c_b_ig, c_lam, c_w_out):
    bp, s_len, _ = x_prompt.shape
    bs = x_sample.shape[0]
    depth = norms.shape[0]
    n_phys, n_ab = cache_k.shape[0], cache_k.shape[1]
    mp = bp * s_len
    tm_x = min(512, s_len)

    xp = x_prompt.reshape(mp, D_MODEL)
    xs = x_sample.reshape(bs, D_MODEL)
    pp = p_prompt.reshape(depth, mp, D_PLE)
    psm = p_sample.reshape(depth, bs, D_PLE)
    cache_kt = jnp.transpose(cache_k, (0, 1, 3, 4, 2)).reshape(n_phys, n_ab, D_A, PAGE_SIZE)
    cache_vt = jnp.transpose(cache_v, (0, 1, 3, 4, 2)).reshape(n_phys, n_ab, D_A, PAGE_SIZE)
    cache_ft = jnp.swapaxes(cache_logf, 2, 3)
    w_pe = ple_w_pe.astype(BF16)
    w_pg = ple_w_pg.astype(BF16)

    kv_p = kv_s = None
    fp_l, fs_l = [], []
    cbp_l, cbs_l, ccp_l, ccs_l, hp_l, hs_l = [], [], [], [], [], []
    for i in range(depth):
        g = norms[i].reshape(norms.shape[1], 1, D_MODEL)
        xs, w_ffn = _ffn_step(xs, g[0], g[1], ffn1_wi, ffn1_wo, i)
        xp = _ffn_half(xp, g[0], g[1], w_ffn, i, tm_x)
        l = i // 2
        if i % 2 == 0:
            w_in = ab_w_in[l]
            wa = jnp.concatenate([w_in[:, :D_A], w_in[:, 3 * D_A + H_A:]], axis=1).astype(BF16)
            wkvf = jnp.pad(w_in[:, D_A:3 * D_A + H_A].T, ((0, BF16_ROWS - H_A), (0, 0))).astype(BF16)
            b_f = ab_b_f[l].reshape(H_A, 1)
            w_out = ab_w_out[l].astype(BF16)
            q, kt, vt, ktb, vtb, lft, bg, cx = _ab_proj(xp, g[2], wa, wkvf, b_f, bp, tm_x, HD_A ** -0.5 * LOG2E,
                                                        l, n_ab, kv_p)
            kv_p = (kt, vt)
            nb = s_len // tm_x
            ck = _cumsum_lanes(lft).reshape(bp, N_PAIR, 2, nb, tm_x).transpose(0, 1, 3, 2, 4)
            ya = _fox_prompt(q, ktb, vtb, ck)
            mix = (g[3], ya, bg, cx, ab_conv_w[l], w_out, s_len)
            fp_l.append(lft)
            cbp_l.append(cx.reshape(bp, s_len, D_B)[:, s_len - (CONV_B - 1):])
            q, kt, vt, _, _, lft, bg, cx = _ab_proj(xs, g[2], wa, wkvf, b_f, 1, bs, HD_A ** -0.5, l, n_ab, kv_s)
            kv_s = (kt, vt)
            ya = _fox_decode(page_table, q.reshape(bs, 1, D_A), kt[0, l].T.reshape(bs, 1, D_A),
                             vt[0, l].T.reshape(bs, 1, D_A), lft[0].T.reshape(bs, H_A, 1),
                             cache_kt, cache_vt, cache_ft, l)
            st = state_conv_b[:, l]
            xs = _ab_step_out(xs, g[3], ya.reshape(bs, D_A), bg, cx, ab_conv_w[l], w_out, (st[:, 0], st[:, 1]))
            fs_l.append(lft[0])
            cbs_l.append(jnp.stack([st[:, 1], cx], axis=1))
        else:
            w_in = c_w_in[l].astype(BF16)
            wgi = jnp.concatenate([c_w_rg[l], c_w_ig[l]], axis=-1).astype(BF16)
            row = lambda a: a.reshape(1, D_RNN)
            gate_args = (c_conv_w[l], row(c_conv_b[l]), wgi, row(c_b_rg[l]), row(c_b_ig[l]), row(c_lam[l]),
                         c_w_out[l].astype(BF16))
            xp, xr_tail, h_tail = _c_mix(xp, g[2], g[3], w_in, *gate_args, bp, tm_x)
            mix = None
            ccp_l.append(xr_tail[:, SUBLANES - (CONV_C - 1):])
            hp_l.append(h_tail[:, SUBLANES - 1])
            st = state_conv_c[:, l]
            xs, xr, h = _c_step(xs, g[2], g[3], w_in, (st[:, 0], st[:, 1], st[:, 2]), state_h_c[:, l], *gate_args)
            ccs_l.append(jnp.stack([st[:, 1], st[:, 2], xr], axis=1))
            hs_l.append(h)
        xs, w_ffn = _ffn_step(xs, g[4], g[5], ffn2_wi, ffn2_wo, i, ple=(psm, g[6], w_pe, w_pg))
        xp = _ffn_half(xp, g[4], g[5], w_ffn, i, tm_x, ple=(pp, g[6], w_pe, w_pg), mix=mix)

    st = lambda lst: jnp.stack(lst, axis=1)
    out_p = lambda a: a.reshape(bp, n_ab, H_A, HD_A, s_len).transpose(0, 1, 4, 2, 3)
    out_s = lambda a: a.reshape(n_ab, 1, H_A, HD_A, bs).transpose(4, 0, 1, 2, 3)
    return (xp.reshape(bp, s_len, D_MODEL), xs.reshape(bs, 1, D_MODEL),
            out_p(kv_p[0]), out_p(kv_p[1]), st(fp_l).transpose(0, 1, 3, 2),
            out_s(kv_s[0]), out_s(kv_s[1]), jnp.stack(fs_l, axis=0).reshape(n_ab, 1, H_A, bs).transpose(3, 0, 1, 2),
            st(cbp_l), st(cbs_l), st(ccp_l), st(ccs_l), st(hp_l), st(hs_l))
```

```python
import functools

import jax
import jax.numpy as jnp
from jax import lax
from jax.experimental import pallas as pl
from jax.experimental.pallas import tpu as pltpu

F32 = jnp.float32
BF16 = jnp.bfloat16

D_MODEL = 1024
D_PLE = 256
H_A = 8
HD_A = 64
D_A = H_A * HD_A
D_B = D_MODEL - D_A
CONV_B = 3
D_RNN = D_MODEL
N_BLK_C = 8
BS_C = D_RNN // N_BLK_C
CONV_C = 4
LRU_C = 8.0
D_FF = 2816
EPS = 1e-6
NEG = -1e30
PAGE_SIZE = 128

LANES = 128
SUBLANES = 8
BF16_ROWS = 16
FF_CHUNKS = ((0, 1024), (1024, 1024), (2048, 768))
FF_STEP = 256
FFN_ROWS = 512
LOG2E = 1.4426950408889634
DECODE_SEQS = 2
N_PAIR = D_A // LANES
KVF_ROWS = 2 * D_A + BF16_ROWS
VMEM_LIMIT = 56 * 1024 * 1024

_NT = (((1,), (1,)), ((), ()))


def _cparams(*sem):
    return pltpu.CompilerParams(dimension_semantics=sem, vmem_limit_bytes=VMEM_LIMIT)


def _rms(x, g):
    return x * lax.rsqrt(jnp.mean(x * x, axis=-1, keepdims=True) + EPS) * g


def _dot(a, b):
    return jnp.dot(a, b, preferred_element_type=F32)


def _dot_nt(a, b):
    return lax.dot_general(a, b, _NT, preferred_element_type=F32)


def _log_sigmoid(x):
    return jnp.minimum(x, 0.0) - jnp.log1p(jnp.exp(-jnp.abs(x)))


def _softplus(x):
    return jnp.maximum(x, 0.0) + jnp.log1p(jnp.exp(-jnp.abs(x)))


def _sigmoid(x):
    return 1.0 / (1.0 + jnp.exp(-x))


def _gelu_tanh(x):
    c = 0.7978845608028654
    return 0.5 * x * (1.0 + jnp.tanh(c * (x + 0.044715 * (x * x * x))))


def _row_spec(tm, d):
    return pl.BlockSpec((tm, d), lambda i: (i, 0))


def _full_spec(shape):
    nd = len(shape)
    return pl.BlockSpec(shape, lambda i: (0,) * nd)


def _shifted_rows(cur, prev, n_shift):
    sub = lax.broadcasted_iota(jnp.int32, (SUBLANES, 1), 0)
    out = []
    for sft in range(1, n_shift + 1):
        r = pltpu.roll(cur, sft, 0)
        top = r[0:SUBLANES, :]
        for i in range(sft):
            top = jnp.where(sub == i, prev[SUBLANES - sft + i:SUBLANES - sft + i + 1, :], top)
        out.append(jnp.concatenate([top, r[SUBLANES:, :]], axis=0))
    return out


def _short_conv_out(x, ya, bg, cx, m1, m2, cw_ref, wo_ref, g):
    conv = m2 * cw_ref[0:1, :] + m1 * cw_ref[1:2, :] + cx * cw_ref[2:3, :]
    yb = bg * conv
    y = _dot(ya, wo_ref[0:D_A, :]) + _dot(yb.astype(BF16), wo_ref[D_A:, :])
    return x + _rms(y, g)


def _ffn_kernel(*refs, with_mix, with_ple, tiles_per_seq):
    refs = list(refs)
    x = refs.pop(0)[...]
    if with_mix:
        gmix_ref, ya_ref, bg_ref, cx_ref, prev_ref, cw_ref, wmix_ref = refs[:7]
        del refs[:7]
        cx = cx_ref[...]
        first = (pl.program_id(0) % tiles_per_seq) == 0
        m1, m2 = _shifted_rows(cx, prev_ref[...] * jnp.where(first, 0.0, 1.0), CONV_B - 1)
        x = _short_conv_out(x, ya_ref[...], bg_ref[...], cx, m1, m2, cw_ref, wmix_ref, gmix_ref[...])
    gpre_ref, gpost_ref, wig_ref, wiu_ref, wo_ref = refs[:5]
    del refs[:5]
    n_grp = max(1, x.shape[0] // FFN_ROWS)
    xg = [x[r * FFN_ROWS:(r + 1) * FFN_ROWS] for r in range(n_grp)] if n_grp > 1 else [x]
    xn = [_rms(xr, gpre_ref[...]).astype(BF16) for xr in xg]
    acc = [None] * n_grp
    for c0, cw in FF_CHUNKS:
        for r in range(n_grp):
            g = _dot(xn[r], wig_ref[:, c0:c0 + cw])
            u = _dot(xn[r], wiu_ref[:, c0:c0 + cw])
            h = (g * _sigmoid(g)) * u
            part = _dot(h.astype(BF16), wo_ref[c0:c0 + cw, :])
            acc[r] = part if acc[r] is None else acc[r] + part
    xg = [xr + 0.5 * _rms(a, gpost_ref[...]) for xr, a in zip(xg, acc)]
    x = jnp.concatenate(xg, axis=0) if n_grp > 1 else xg[0]
    if with_ple:
        x = _ple(x, *refs[:4])
        del refs[:4]
    o_ref, = refs
    o_ref[...] = x


def _ple(x, p_ref, g_ref, wpe_ref, wpg_ref):
    gate = _sigmoid(_dot(_rms(x, g_ref[...]).astype(BF16), wpg_ref[...]))
    return x + _dot(p_ref[...].astype(BF16), wpe_ref[...]) * gate


def _ple_specs(layer, tm, index):
    resident = lambda shape: pl.BlockSpec((None,) + shape, lambda *_: (layer, 0, 0), pipeline_mode=pl.Buffered(1))
    return [pl.BlockSpec((None, tm, D_PLE), lambda *i: (layer,) + index(*i)),
            pl.BlockSpec((1, D_MODEL), lambda *_: (0, 0)), resident((D_PLE, D_MODEL)), resident((D_MODEL, D_MODEL))]


def _ffn_half(x, g_pre, g_post, w, layer, tm, ple=None, mix=None):
    m = x.shape[0]
    resident = lambda shape: pl.BlockSpec(shape, lambda i: (0, 0), pipeline_mode=pl.Buffered(1))
    in_specs = [_row_spec(tm, D_MODEL)]
    args = [x]
    tiles_per_seq = 1
    if mix is not None:
        g_mix, ya, bg, cx, conv_w, w_out, seq_len = mix
        tiles_per_seq = seq_len // tm
        per = tm // SUBLANES
        in_specs += [_full_spec((1, D_MODEL)), _row_spec(tm, D_A), _row_spec(tm, D_B), _row_spec(tm, D_B),
                     pl.BlockSpec((SUBLANES, D_B), lambda i: (jnp.maximum(i * per - 1, 0), 0)),
                     _full_spec((CONV_B, D_B)), _full_spec((D_MODEL, D_MODEL))]
        args += [g_mix, ya, bg, cx, cx, conv_w, w_out]
    in_specs += [_full_spec((1, D_MODEL)), _full_spec((1, D_MODEL)),
                 resident((D_MODEL, D_FF)), resident((D_MODEL, D_FF)), resident((D_FF, D_MODEL))]
    args += [g_pre, g_post, *w]
    if ple is not None:
        in_specs += _ple_specs(layer, tm, lambda i: (i, 0))
        args += list(ple)
    return pl.pallas_call(
        functools.partial(_ffn_kernel, with_mix=mix is not None, with_ple=ple is not None,
                          tiles_per_seq=tiles_per_seq),
        grid=(m // tm,),
        in_specs=in_specs,
        out_specs=_row_spec(tm, D_MODEL),
        out_shape=jax.ShapeDtypeStruct((m, D_MODEL), F32),
        compiler_params=_cparams("parallel"),
        name="ffn_half",
    )(*args)


def _ffn_step_kernel(*refs, with_ple):
    refs = list(refs)
    x_ref, gpre_ref, gpost_ref, wig_ref, wiu_ref, wo_ref = refs[:6]
    ple_refs = refs[6:10] if with_ple else []
    o_ref, wigb_ref, wiub_ref, wob_ref, xn_ref, acc_ref = refs[-6:]
    j = pl.program_id(0)

    @pl.when(j == 0)
    def _():
        xn_ref[...] = _rms(x_ref[...], gpre_ref[...]).astype(BF16)
        acc_ref[...] = jnp.zeros_like(acc_ref)

    wig = wig_ref[...].astype(BF16)
    wiu = wiu_ref[...].astype(BF16)
    wo = wo_ref[...].astype(BF16)
    wigb_ref[...] = wig
    wiub_ref[...] = wiu
    wob_ref[...] = wo
    xn = xn_ref[...]
    g = _dot(xn, wig)
    u = _dot(xn, wiu)
    acc_ref[...] += _dot(((g * _sigmoid(g)) * u).astype(BF16), wo)

    @pl.when(j == pl.num_programs(0) - 1)
    def _():
        x = x_ref[...] + 0.5 * _rms(acc_ref[...], gpost_ref[...])
        if with_ple:
            x = _ple(x, *ple_refs)
        o_ref[...] = x


def _ffn_step(x, g_pre, g_post, wi, wo, layer, ple=None):
    m = x.shape[0]
    nf = D_FF // FF_STEP
    const = lambda shape: pl.BlockSpec(shape, lambda j: (0,) * len(shape))
    in_specs = [const((m, D_MODEL)), const((1, D_MODEL)), const((1, D_MODEL)),
                pl.BlockSpec((None, D_MODEL, FF_STEP), lambda j: (layer, 0, j)),
                pl.BlockSpec((None, D_MODEL, FF_STEP), lambda j: (layer, 0, j + nf)),
                pl.BlockSpec((None, FF_STEP, D_MODEL), lambda j: (layer, j, 0))]
    args = [x, g_pre, g_post, wi, wi, wo]
    if ple is not None:
        in_specs += _ple_specs(layer, m, lambda j: (0, 0))
        args += list(ple)
    up_spec = pl.BlockSpec((D_MODEL, FF_STEP), lambda j: (0, j))
    up_shape = jax.ShapeDtypeStruct((D_MODEL, D_FF), BF16)
    out = pl.pallas_call(
        functools.partial(_ffn_step_kernel, with_ple=ple is not None),
        grid=(nf,),
        in_specs=in_specs,
        out_specs=[const((m, D_MODEL)), up_spec, up_spec, pl.BlockSpec((FF_STEP, D_MODEL), lambda j: (j, 0))],
        out_shape=[jax.ShapeDtypeStruct((m, D_MODEL), F32), up_shape, up_shape,
                   jax.ShapeDtypeStruct((D_FF, D_MODEL), BF16)],
        scratch_shapes=[pltpu.VMEM((m, D_MODEL), BF16), pltpu.VMEM((m, D_MODEL), F32)],
        compiler_params=_cparams("arbitrary"),
        name="ffn_step",
    )(*args)
    return out[0], tuple(out[1:])


def _ab_proj_kernel(*refs, q_scale):
    x_ref, g_ref, wa_ref, wkvf_ref, bf_ref = refs[:5]
    q_ref, kt_ref, vt_ref, ktb_ref, vtb_ref, lft_ref, bg_ref, cx_ref = refs[-8:]
    tm = x_ref.shape[0]
    xn = _rms(x_ref[...], g_ref[...]).astype(BF16)
    col = lambda c: _dot(xn, wa_ref[:, c * D_A:(c + 1) * D_A])
    q_ref[...] = (col(0) * q_scale).astype(BF16)
    bg_ref[...] = col(1)
    cx_ref[...] = col(2) * col(3)
    kvf = _dot_nt(wkvf_ref[...], xn)
    kt = kvf[0:D_A]
    vt = kvf[D_A:2 * D_A]
    kt_ref[...] = kt
    vt_ref[...] = vt
    ktb_ref[...] = kt.astype(BF16).reshape(N_PAIR, LANES, tm)
    vtb_ref[:, 0:HD_A, :] = vt.astype(BF16).reshape(H_A, HD_A, tm)
    ones_row = lax.broadcasted_iota(jnp.int32, (H_A, LANES - HD_A, tm), 1) == 0
    vtb_ref[:, HD_A:, :] = jnp.where(ones_row, 1.0, 0.0).astype(BF16)
    lft_ref[...] = _log_sigmoid(kvf[2 * D_A:2 * D_A + H_A] + bf_ref[...])


def _ab_proj(x, g, wa, wkvf, b_f, n_seq, tm, q_scale, layer, n_layers, kv_prev=None):
    m = x.shape[0]
    s_len = m // n_seq
    nt = s_len // tm
    rows = lambda d: pl.BlockSpec((tm, d), lambda b, t: (b * nt + t, 0))
    full = lambda shape: pl.BlockSpec(shape, lambda b, t: (0,) * len(shape))
    kv_spec = pl.BlockSpec((None, None, D_A, tm), lambda b, t: (b, layer, 0, t))
    kv_shape = jax.ShapeDtypeStruct((n_seq, n_layers, D_A, s_len), F32)
    tile_spec = lambda n: pl.BlockSpec((None, n, None, LANES, tm), lambda b, t: (b, 0, t, 0, 0))
    wide = lambda dt: jax.ShapeDtypeStruct((m, D_A), dt)
    tile_shape = lambda n: jax.ShapeDtypeStruct((n_seq, n, nt, LANES, tm), BF16)
    in_specs = [rows(D_MODEL), full((1, D_MODEL)), full((D_MODEL, 4 * D_A)), full((KVF_ROWS, D_MODEL)),
                full((H_A, 1))]
    args = [x, g, wa, wkvf, b_f]
    aliases = {}
    if kv_prev is not None:
        in_specs += [pl.BlockSpec(memory_space=pl.ANY)] * 2
        args += list(kv_prev)
        aliases = {5: 1, 6: 2}
    return pl.pallas_call(
        functools.partial(_ab_proj_kernel, q_scale=q_scale),
        grid=(n_seq, nt),
        in_specs=in_specs,
        out_specs=[rows(D_A), kv_spec, kv_spec, tile_spec(N_PAIR), tile_spec(H_A),
                   pl.BlockSpec((None, H_A, tm), lambda b, t: (b, 0, t)), rows(D_B), rows(D_B)],
        out_shape=[wide(BF16), kv_shape, kv_shape, tile_shape(N_PAIR), tile_shape(H_A),
                   jax.ShapeDtypeStruct((n_seq, H_A, s_len), F32), wide(F32), wide(F32)],
        input_output_aliases=aliases,
        compiler_params=_cparams("parallel", "parallel"),
        name="ab_proj",
    )(*args)


def _cumsum_kernel(x_ref, o_ref):
    x = x_ref[...]
    n = x.shape[1]
    pos = lax.broadcasted_iota(jnp.int32, (1, n), 1)
    d = 1
    while d < n:
        x = x + jnp.where(pos >= d, pltpu.roll(x, d, 1), 0.0)
        d *= 2
    o_ref[...] = x * LOG2E


def _cumsum_lanes(x):
    spec = pl.BlockSpec((None,) + x.shape[1:], lambda b: (b, 0, 0))
    return pl.pallas_call(
        _cumsum_kernel,
        grid=(x.shape[0],),
        in_specs=[spec],
        out_specs=spec,
        out_shape=jax.ShapeDtypeStruct(x.shape, F32),
        compiler_params=_cparams("parallel"),
        name="logf_cumsum",
    )(x)


def _fox_prompt_kernel(q_ref, kt_ref, vt_ref, ck_ref, o_ref):
    qi = pl.program_id(2)
    t = q_ref.shape[0]
    lane = lax.broadcasted_iota(jnp.int32, (1, LANES), 1)
    q = q_ref[...]
    zero = jnp.zeros_like(q)
    qs = (jnp.where(lane < HD_A, q, zero), jnp.where(lane >= HD_A, q, zero))

    def tile(j, state, diagonal):
        kt = kt_ref[j]
        ck = ck_ref[j]
        out = []
        scores = [_dot(qs[e], kt) - ck[e:e + 1, :] for e in range(2)]
        for e in range(2):
            m_prev, acc_prev = state[e]
            s = scores[e]
            if diagonal:
                row = lax.broadcasted_iota(jnp.int32, (t, 1), 0)
                colk = lax.broadcasted_iota(jnp.int32, (1, t), 1)
                s = jnp.where(colk <= row, s, NEG)
            m_new = jnp.maximum(m_prev, jnp.max(s, axis=-1, keepdims=True))
            p = jnp.exp2(s - m_new).astype(BF16)
            acc = jnp.exp2(m_prev - m_new) * acc_prev + _dot_nt(p, vt_ref[e, j])
            out.append((m_new, acc))
        return tuple(out)

    init = tuple((jnp.full((t, 1), NEG, F32), jnp.zeros((t, LANES), F32)) for _ in range(2))
    state = lax.fori_loop(0, qi // 2, lambda i, st: tile(2 * i + 1, tile(2 * i, st, False), False), init)
    (_, a0), (_, a1) = lax.cond(qi % 2 == 1, lambda st: tile(qi, tile(qi - 1, st, False), True),
                                lambda st: tile(qi, st, True), state)
    o0 = a0[:, 0:HD_A] / a0[:, HD_A:HD_A + 1]
    o1 = a1[:, 0:HD_A] / a1[:, HD_A:HD_A + 1]
    o_ref[...] = jnp.concatenate([o0, o1], axis=-1).astype(BF16)


def _fox_prompt(q, ktb, vtb, ck):
    n_seq, _, nb, _, t = ktb.shape
    return pl.pallas_call(
        _fox_prompt_kernel,
        grid=(n_seq, N_PAIR, nb),
        in_specs=[pl.BlockSpec((t, LANES), lambda b, h, i: (b * nb + i, h)),
                  pl.BlockSpec((None, None, nb, LANES, t), lambda b, h, i: (b, h, 0, 0, 0)),
                  pl.BlockSpec((None, 2, nb, LANES, t), lambda b, h, i: (b, h, 0, 0, 0)),
                  pl.BlockSpec((None, None, nb, 2, t), lambda b, h, i: (b, h, 0, 0, 0))],
        out_specs=pl.BlockSpec((t, LANES), lambda b, h, i: (b * nb + i, h)),
        out_shape=jax.ShapeDtypeStruct(q.shape, BF16),
        compiler_params=_cparams("parallel", "parallel", "parallel"),
        name="fox_prompt",
    )(q, ktb, vtb, ck)


def _fox_decode_kernel(pt_ref, q_ref, kn_ref, vn_ref, fn_ref, ft_ref, *refs, n_pages, n_seq):
    k_refs = refs[:n_seq * n_pages]
    v_refs = refs[n_seq * n_pages:2 * n_seq * n_pages]
    o_ref = refs[2 * n_seq * n_pages]
    ps = PAGE_SIZE
    b0 = pl.program_id(0) * n_seq

    lane = lax.broadcasted_iota(jnp.int32, (H_A, D_A), 1)
    hrow = lax.broadcasted_iota(jnp.int32, (H_A, D_A), 0)
    own = (lane // HD_A) == hrow
    pos = lax.broadcasted_iota(jnp.int32, (H_A, ps), 1)

    for sq in range(n_seq):
        qbd_f = jnp.where(own, q_ref[sq].astype(F32), 0.0)
        qbd = qbd_f.astype(BF16)

        carry = fn_ref[sq]
        bias = [None] * n_pages
        for j in reversed(range(n_pages)):
            lf = ft_ref[pt_ref[b0 + sq, j]]
            x = lf
            d = 1
            while d < ps:
                x = x + jnp.where(pos + d < ps, pltpu.roll(x, ps - d, 1), 0.0)
                d *= 2
            bias[j] = (x - lf) + carry
            carry = carry + x[:, 0:1]

        kn = kn_ref[sq].astype(BF16).astype(F32)
        s_new = jnp.sum(qbd_f * kn, axis=-1, keepdims=True)
        scores = []
        m = s_new
        for j in range(n_pages):
            s = _dot(qbd, k_refs[sq * n_pages + j][...].astype(BF16)) + bias[j]
            scores.append(s)
            m = jnp.maximum(m, jnp.max(s, axis=-1, keepdims=True))
        p_new = jnp.exp(s_new - m)
        l = p_new
        vn = vn_ref[sq].astype(BF16).astype(F32)
        acc = p_new.astype(BF16).astype(F32) * vn
        for j in range(n_pages):
            p = jnp.exp(scores[j] - m)
            l = l + jnp.sum(p, axis=-1, keepdims=True)
            acc = acc + _dot_nt(p.astype(BF16), v_refs[sq * n_pages + j][...].astype(BF16))
        o = jnp.where(own, acc / l, 0.0)
        o_ref[sq] = jnp.sum(o, axis=0, keepdims=True).astype(BF16)


def _fox_decode(page_table, q, k_new, v_new, f_new, cache_kt, cache_vt, cache_ft, layer):
    bsz, n_pages = page_table.shape
    n_seq = DECODE_SEQS
    n_phys = cache_ft.shape[0]
    row = lambda dt_shape: pl.BlockSpec((n_seq,) + dt_shape, lambda b, pt: (b, 0, 0))
    kv_specs = [pl.BlockSpec((None, None, D_A, PAGE_SIZE), functools.partial(
        lambda b, pt, sq, j: (pt[b * n_seq + sq, j], layer, 0, 0), sq=sq, j=j))
        for sq in range(n_seq) for j in range(n_pages)]
    f_spec = pl.BlockSpec((n_phys, None, H_A, PAGE_SIZE), lambda b, pt: (0, layer, 0, 0),
                          pipeline_mode=pl.Buffered(1))
    grid_spec = pltpu.PrefetchScalarGridSpec(
        num_scalar_prefetch=1,
        grid=(bsz // n_seq,),
        in_specs=[row((1, D_A)), row((1, D_A)), row((1, D_A)), row((H_A, 1)), f_spec] + kv_specs + kv_specs,
        out_specs=row((1, D_A)),
    )
    n_kv = n_seq * n_pages
    return pl.pallas_call(
        functools.partial(_fox_decode_kernel, n_pages=n_pages, n_seq=n_seq),
        grid_spec=grid_spec,
        out_shape=jax.ShapeDtypeStruct((bsz, 1, D_A), BF16),
        compiler_params=_cparams("parallel"),
        name="fox_decode",
    )(page_table, q, k_new, v_new, f_new, cache_ft, *([cache_kt] * n_kv), *([cache_vt] * n_kv))


def _ab_step_out_kernel(x_ref, g_ref, ya_ref, bg_ref, cx_ref, m1_ref, m2_ref, cw_ref, wo_ref, o_ref):
    o_ref[...] = _short_conv_out(x_ref[...], ya_ref[...], bg_ref[...], cx_ref[...], m1_ref[...], m2_ref[...],
                                 cw_ref, wo_ref, g_ref[...])


def _ab_step_out(x, g, ya, bg, cx, conv_w, w_out, state):
    m = x.shape[0]
    return pl.pallas_call(
        _ab_step_out_kernel,
        grid=(1,),
        in_specs=[_row_spec(m, D_MODEL), _full_spec((1, D_MODEL)), _row_spec(m, D_A), _row_spec(m, D_B),
                  _row_spec(m, D_B), _row_spec(m, D_B), _row_spec(m, D_B), _full_spec((CONV_B, D_B)),
                  _full_spec((D_MODEL, D_MODEL))],
        out_specs=_row_spec(m, D_MODEL),
        out_shape=jax.ShapeDtypeStruct((m, D_MODEL), F32),
        compiler_params=_cparams("arbitrary"),
        name="ab_step_out",
    )(x, g, ya, bg, cx, state[1], state[0], conv_w, w_out)


def _lru_gates(xc, wgi_ref, brg_ref, big_ref, lam_ref):
    r_parts, i_parts = [], []
    for n in range(N_BLK_C):
        ri = _dot(xc[:, n * BS_C:(n + 1) * BS_C].astype(BF16), wgi_ref[n])
        r_parts.append(ri[:, :BS_C])
        i_parts.append(ri[:, BS_C:])
    r = _sigmoid(jnp.concatenate(r_parts, axis=-1) + brg_ref[...])
    ig = _sigmoid(jnp.concatenate(i_parts, axis=-1) + big_ref[...])
    log_a = -LRU_C * r * _softplus(-lam_ref[...])
    a = jnp.exp(log_a)
    th = jnp.tanh(log_a)
    u = jnp.sqrt(-2.0 * th / (1.0 - th)) * (ig * xc)
    return a, u


def _c_mix_kernel(x_ref, g2_ref, g3_ref, win_ref, cw_ref, cb_ref, wgi_ref, brg_ref, big_ref, lam_ref, wout_ref,
                  o_ref, xr_tail_ref, h_tail_ref, prev_ref, h_ref, a_ref, u_ref):
    tm = x_ref.shape[0]
    ngrp = tm // SUBLANES

    @pl.when(pl.program_id(1) == 0)
    def _():
        prev_ref[...] = jnp.zeros_like(prev_ref)
        h_ref[...] = jnp.zeros_like(h_ref)

    x = x_ref[...]
    xn = _rms(x, g2_ref[...]).astype(BF16)
    gate = _dot(xn, win_ref[:, 0:D_RNN])
    xr = _dot(xn, win_ref[:, D_RNN:])
    m1, m2, m3 = _shifted_rows(xr, prev_ref[...], CONV_C - 1)
    xc = m3 * cw_ref[0:1, :] + m2 * cw_ref[1:2, :] + m1 * cw_ref[2:3, :] + xr * cw_ref[3:4, :] + cb_ref[...]
    a, u = _lru_gates(xc, wgi_ref, brg_ref, big_ref, lam_ref)

    a = a.reshape(ngrp, SUBLANES, D_RNN)
    u = u.reshape(ngrp, SUBLANES, D_RNN)
    sub = lax.broadcasted_iota(jnp.int32, (1, SUBLANES, 1), 1)
    d = 1
    while d < SUBLANES:
        ok = sub >= d
        a_sh = jnp.where(ok, pltpu.roll(a, d, 1), 1.0)
        u_sh = jnp.where(ok, pltpu.roll(u, d, 1), 0.0)
        u = a * u_sh + u
        a = a * a_sh
        d *= 2
    a_ref[...] = a
    u_ref[...] = u

    def group(i, h):
        hs = a_ref[i] * h + u_ref[i]
        u_ref[i] = hs
        return hs[SUBLANES - 1:SUBLANES, :]

    h_ref[...] = lax.fori_loop(0, ngrp, group, h_ref[...], unroll=4)
    hs = u_ref[...].reshape(tm, D_RNN)
    z = _gelu_tanh(gate) * hs
    o_ref[...] = x + _rms(_dot(z.astype(BF16), wout_ref[...]), g3_ref[...])
    prev_ref[...] = xr[tm - SUBLANES:, :]
    xr_tail_ref[...] = xr[tm - SUBLANES:, :]
    h_tail_ref[...] = hs[tm - SUBLANES:, :]


def _c_mix(x, g2, g3, w_in, conv_w, conv_b, wgi, b_rg, b_ig, lam, w_out, n_seq, tm):
    m = x.shape[0]
    nt = m // n_seq // tm
    rows = pl.BlockSpec((tm, D_MODEL), lambda b, t: (b * nt + t, 0))
    full = lambda shape: pl.BlockSpec(shape, lambda b, t: (0,) * len(shape))
    tail = pl.BlockSpec((None, SUBLANES, D_RNN), lambda b, t: (b, 0, 0))
    tail_shape = jax.ShapeDtypeStruct((n_seq, SUBLANES, D_RNN), F32)
    return pl.pallas_call(
        _c_mix_kernel,
        grid=(n_seq, nt),
        in_specs=[rows, full((1, D_MODEL)), full((1, D_MODEL)), full((D_MODEL, 2 * D_RNN)), full((CONV_C, D_RNN)),
                  full((1, D_RNN)), full((N_BLK_C, BS_C, 2 * BS_C)), full((1, D_RNN)), full((1, D_RNN)),
                  full((1, D_RNN)), full((D_RNN, D_MODEL))],
        out_specs=[rows, tail, tail],
        out_shape=[jax.ShapeDtypeStruct((m, D_MODEL), F32), tail_shape, tail_shape],
        scratch_shapes=[pltpu.VMEM((SUBLANES, D_RNN), F32), pltpu.VMEM((1, D_RNN), F32),
                        pltpu.VMEM((tm // SUBLANES, SUBLANES, D_RNN), F32),
                        pltpu.VMEM((tm // SUBLANES, SUBLANES, D_RNN), F32)],
        compiler_params=_cparams("parallel", "arbitrary"),
        name="c_mix",
    )(x, g2, g3, w_in, conv_w, conv_b, wgi, b_rg, b_ig, lam, w_out)


def _c_step_kernel(x_ref, g2_ref, g3_ref, win_ref, s0_ref, s1_ref, s2_ref, h0_ref, cw_ref, cb_ref, wgi_ref,
                   brg_ref, big_ref, lam_ref, wout_ref, o_ref, xr_ref, h_ref):
    x = x_ref[...]
    xn = _rms(x, g2_ref[...]).astype(BF16)
    gate = _dot(xn, win_ref[:, 0:D_RNN])
    xr = _dot(xn, win_ref[:, D_RNN:])
    xc = (s0_ref[...] * cw_ref[0:1, :] + s1_ref[...] * cw_ref[1:2, :] + s2_ref[...] * cw_ref[2:3, :]
          + xr * cw_ref[3:4, :] + cb_ref[...])
    a, u = _lru_gates(xc, wgi_ref, brg_ref, big_ref, lam_ref)
    h = a * h0_ref[...] + u
    z = _gelu_tanh(gate) * h
    o_ref[...] = x + _rms(_dot(z.astype(BF16), wout_ref[...]), g3_ref[...])
    xr_ref[...] = xr
    h_ref[...] = h


def _c_step(x, g2, g3, w_in, state, h0, conv_w, conv_b, wgi, b_rg, b_ig, lam, w_out):
    m = x.shape[0]
    rows = _row_spec(m, D_RNN)
    shape = jax.ShapeDtypeStruct((m, D_RNN), F32)
    return pl.pallas_call(
        _c_step_kernel,
        grid=(1,),
        in_specs=[rows, _full_spec((1, D_MODEL)), _full_spec((1, D_MODEL)), _full_spec((D_MODEL, 2 * D_RNN)),
                  rows, rows, rows, rows, _full_spec((CONV_C, D_RNN)), _full_spec((1, D_RNN)),
                  _full_spec((N_BLK_C, BS_C, 2 * BS_C)), _full_spec((1, D_RNN)), _full_spec((1, D_RNN)),
                  _full_spec((1, D_RNN)), _full_spec((D_RNN, D_MODEL))],
        out_specs=[rows, rows, rows],
        out_shape=[shape, shape, shape],
        compiler_params=_cparams("arbitrary"),
        name="c_step",
    )(x, g2, g3, w_in, state[0], state[1], state[2], h0, conv_w, conv_b, wgi, b_rg, b_ig, lam, w_out)


def kernel(x_prompt, x_sample, p_prompt, p_sample, cache_k, cache_v, cache_logf, state_conv_b, state_conv_c, state_h_c, page_table, norms, ffn1_wi, ffn1_wo, ffn2_wi, ffn2_wo, ple_w_pe, ple_w_pg, ab_w_in, ab_b_f, ab_conv_w, ab_w_out, c_w_in, c_conv_w, c_conv_b, c_w_rg, c_b_rg, c_w_ig, c_b_ig, c_lam, c_w_out):
    bp, s_len, _ = x_prompt.shape
    bs = x_sample.shape[0]
    depth = norms.shape[0]
    n_phys, n_ab = cache_k.shape[0], cache_k.shape[1]
    mp = bp * s_len
    tm_x = min(512, s_len)

    xp = x_prompt.reshape(mp, D_MODEL)
    xs = x_sample.reshape(bs, D_MODEL)
    pp = p_prompt.reshape(depth, mp, D_PLE)
    psm = p_sample.reshape(depth, bs, D_PLE)
    cache_kt = jnp.transpose(cache_k, (0, 1, 3, 4, 2)).reshape(n_phys, n_ab, D_A, PAGE_SIZE)
    cache_vt = jnp.transpose(cache_v, (0, 1, 3, 4, 2)).reshape(n_phys, n_ab, D_A, PAGE_SIZE)
    cache_ft = jnp.swapaxes(cache_logf, 2, 3)
    w_pe = ple_w_pe.astype(BF16)
    w_pg = ple_w_pg.astype(BF16)

    kv_p = kv_s = None
    fp_l, fs_l = [], []
    cbp_l, cbs_l, ccp_l, ccs_l, hp_l, hs_l = [], [], [], [], [], []
    for i in range(depth):
        g = norms[i].reshape(norms.shape[1], 1, D_MODEL)
        xs, w_ffn = _ffn_step(xs, g[0], g[1], ffn1_wi, ffn1_wo, i)
        xp = _ffn_half(xp, g[0], g[1], w_ffn, i, 2 * tm_x)
        l = i // 2
        if i % 2 == 0:
            w_in = ab_w_in[l]
            wa = jnp.concatenate([w_in[:, :D_A], w_in[:, 3 * D_A + H_A:]], axis=1).astype(BF16)
            wkvf = jnp.pad(w_in[:, D_A:3 * D_A + H_A].T, ((0, BF16_ROWS - H_A), (0, 0))).astype(BF16)
            b_f = ab_b_f[l].reshape(H_A, 1)
            w_out = ab_w_out[l].astype(BF16)
            q, kt, vt, ktb, vtb, lft, bg, cx = _ab_proj(xp, g[2], wa, wkvf, b_f, bp, tm_x, HD_A ** -0.5 * LOG2E,
                                                        l, n_ab, kv_p)
            kv_p = (kt, vt)
            nb = s_len // tm_x
            ck = _cumsum_lanes(lft).reshape(bp, N_PAIR, 2, nb, tm_x).transpose(0, 1, 3, 2, 4)
            ya = _fox_prompt(q, ktb, vtb, ck)
            mix = (g[3], ya, bg, cx, ab_conv_w[l], w_out, s_len)
            fp_l.append(lft)
            cbp_l.append(cx.reshape(bp, s_len, D_B)[:, s_len - (CONV_B - 1):])
            q, kt, vt, _, _, lft, bg, cx = _ab_proj(xs, g[2], wa, wkvf, b_f, 1, bs, HD_A ** -0.5, l, n_ab, kv_s)
            kv_s = (kt, vt)
            ya = _fox_decode(page_table, q.reshape(bs, 1, D_A), kt[0, l].T.reshape(bs, 1, D_A),
                             vt[0, l].T.reshape(bs, 1, D_A), lft[0].T.reshape(bs, H_A, 1),
                             cache_kt, cache_vt, cache_ft, l)
            st = state_conv_b[:, l]
            xs = _ab_step_out(xs, g[3], ya.reshape(bs, D_A), bg, cx, ab_conv_w[l], w_out, (st[:, 0], st[:, 1]))
            fs_l.append(lft[0])
            cbs_l.append(jnp.stack([st[:, 1], cx], axis=1))
        else:
            w_in = c_w_in[l].astype(BF16)
            wgi = jnp.concatenate([c_w_rg[l], c_w_ig[l]], axis=-1).astype(BF16)
            row = lambda a: a.reshape(1, D_RNN)
            gate_args = (c_conv_w[l], row(c_conv_b[l]), wgi, row(c_b_rg[l]), row(c_b_ig[l]), row(c_lam[l]),
                         c_w_out[l].astype(BF16))
            xp, xr_tail, h_tail = _c_mix(xp, g[2], g[3], w_in, *gate_args, bp, tm_x)
            mix = None
            ccp_l.append(xr_tail[:, SUBLANES - (CONV_C - 1):])
            hp_l.append(h_tail[:, SUBLANES - 1])
            st = state_conv_c[:, l]
            xs, xr, h = _c_step(xs, g[2], g[3], w_in, (st[:, 0], st[:, 1], st[:, 2]), state_h_c[:, l], *gate_args)
            ccs_l.append(jnp.stack([st[:, 1], st[:, 2], xr], axis=1))
            hs_l.append(h)
        xs, w_ffn = _ffn_step(xs, g[4], g[5], ffn2_wi, ffn2_wo, i, ple=(psm, g[6], w_pe, w_pg))
        xp = _ffn_half(xp, g[4], g[5], w_ffn, i, tm_x, ple=(pp, g[6], w_pe, w_pg), mix=mix)

    st = lambda lst: jnp.stack(lst, axis=1)
    out_p = lambda a: a.reshape(bp, n_ab, H_A, HD_A, s_len).transpose(0, 1, 4, 2, 3)
    out_s = lambda a: a.reshape(n_ab, 1, H_A, HD_A, bs).transpose(4, 0, 1, 2, 3)
    return (xp.reshape(bp, s_len, D_MODEL), xs.reshape(bs, 1, D_MODEL),
            out_p(kv_p[0]), out_p(kv_p[1]), st(fp_l).transpose(0, 1, 3, 2),
            out_s(kv_s[0]), out_s(kv_s[1]), jnp.stack(fs_l, axis=0).reshape(n_ab, 1, H_A, bs).transpose(3, 0, 1, 2),
            st(cbp_l), st(cbs_l), st(ccp_l), st(ccs_l), st(hp_l), st(hs_l))
```

```python
import functools

import jax
import jax.numpy as jnp
from jax import lax
from jax.experimental import pallas as pl
from jax.experimental.pallas import tpu as pltpu

F32 = jnp.float32
BF16 = jnp.bfloat16

D_MODEL = 1024
D_PLE = 256
H_A = 8
HD_A = 64
D_A = H_A * HD_A
D_B = D_MODEL - D_A
CONV_B = 3
D_RNN = D_MODEL
N_BLK_C = 8
BS_C = D_RNN // N_BLK_C
CONV_C = 4
LRU_C = 8.0
D_FF = 2816
EPS = 1e-6
NEG = -1e30
PAGE_SIZE = 128

LANES = 128
SUBLANES = 8
BF16_ROWS = 16
FF_CHUNKS = ((0, 1024), (1024, 1024), (2048, 768))
FF_STEP = 256
FFN_ROWS = 256
LOG2E = 1.4426950408889634
DECODE_SEQS = 2
N_PAIR = D_A // LANES
KVF_ROWS = 2 * D_A + BF16_ROWS
VMEM_LIMIT = 56 * 1024 * 1024

_NT = (((1,), (1,)), ((), ()))


def _cparams(*sem):
    return pltpu.CompilerParams(dimension_semantics=sem, vmem_limit_bytes=VMEM_LIMIT)


def _rms(x, g):
    return x * lax.rsqrt(jnp.mean(x * x, axis=-1, keepdims=True) + EPS) * g


def _dot(a, b):
    return jnp.dot(a, b, preferred_element_type=F32)


def _dot_nt(a, b):
    return lax.dot_general(a, b, _NT, preferred_element_type=F32)


def _log_sigmoid(x):
    return jnp.minimum(x, 0.0) - jnp.log1p(jnp.exp(-jnp.abs(x)))


def _softplus(x):
    return jnp.maximum(x, 0.0) + jnp.log1p(jnp.exp(-jnp.abs(x)))


def _sigmoid(x):
    return 1.0 / (1.0 + jnp.exp(-x))


def _gelu_tanh(x):
    c = 0.7978845608028654
    return 0.5 * x * (1.0 + jnp.tanh(c * (x + 0.044715 * (x * x * x))))


def _row_spec(tm, d):
    return pl.BlockSpec((tm, d), lambda i: (i, 0))


def _full_spec(shape):
    nd = len(shape)
    return pl.BlockSpec(shape, lambda i: (0,) * nd)


def _shifted_rows(cur, prev, n_shift):
    sub = lax.broadcasted_iota(jnp.int32, (SUBLANES, 1), 0)
    out = []
    for sft in range(1, n_shift + 1):
        r = pltpu.roll(cur, sft, 0)
        top = r[0:SUBLANES, :]
        for i in range(sft):
            top = jnp.where(sub == i, prev[SUBLANES - sft + i:SUBLANES - sft + i + 1, :], top)
        out.append(jnp.concatenate([top, r[SUBLANES:, :]], axis=0))
    return out


def _short_conv_out(x, ya, bg, cx, m1, m2, cw_ref, wo_ref, g):
    conv = m2 * cw_ref[0:1, :] + m1 * cw_ref[1:2, :] + cx * cw_ref[2:3, :]
    yb = bg * conv
    y = _dot(ya, wo_ref[0:D_A, :]) + _dot(yb.astype(BF16), wo_ref[D_A:, :])
    return x + _rms(y, g)


def _ffn_kernel(*refs, with_mix, with_ple, tiles_per_seq):
    refs = list(refs)
    x = refs.pop(0)[...]
    if with_mix:
        gmix_ref, ya_ref, bg_ref, cx_ref, prev_ref, cw_ref, wmix_ref = refs[:7]
        del refs[:7]
        cx = cx_ref[...]
        first = (pl.program_id(0) % tiles_per_seq) == 0
        m1, m2 = _shifted_rows(cx, prev_ref[...] * jnp.where(first, 0.0, 1.0), CONV_B - 1)
        x = _short_conv_out(x, ya_ref[...], bg_ref[...], cx, m1, m2, cw_ref, wmix_ref, gmix_ref[...])
    gpre_ref, gpost_ref, wig_ref, wiu_ref, wo_ref = refs[:5]
    del refs[:5]
    n_grp = max(1, x.shape[0] // FFN_ROWS)
    xg = [x[r * FFN_ROWS:(r + 1) * FFN_ROWS] for r in range(n_grp)] if n_grp > 1 else [x]
    xn = [_rms(xr, gpre_ref[...]).astype(BF16) for xr in xg]
    acc = [None] * n_grp
    for c0, cw in FF_CHUNKS:
        for r in range(n_grp):
            g = _dot(xn[r], wig_ref[:, c0:c0 + cw])
            u = _dot(xn[r], wiu_ref[:, c0:c0 + cw])
            h = (g * _sigmoid(g)) * u
            part = _dot(h.astype(BF16), wo_ref[c0:c0 + cw, :])
            acc[r] = part if acc[r] is None else acc[r] + part
    xg = [xr + 0.5 * _rms(a, gpost_ref[...]) for xr, a in zip(xg, acc)]
    x = jnp.concatenate(xg, axis=0) if n_grp > 1 else xg[0]
    if with_ple:
        x = _ple(x, *refs[:4])
        del refs[:4]
    o_ref, = refs
    o_ref[...] = x


def _ple(x, p_ref, g_ref, wpe_ref, wpg_ref):
    gate = _sigmoid(_dot(_rms(x, g_ref[...]).astype(BF16), wpg_ref[...]))
    return x + _dot(p_ref[...].astype(BF16), wpe_ref[...]) * gate


def _ple_specs(layer, tm, index):
    resident = lambda shape: pl.BlockSpec((None,) + shape, lambda *_: (layer, 0, 0), pipeline_mode=pl.Buffered(1))
    return [pl.BlockSpec((None, tm, D_PLE), lambda *i: (layer,) + index(*i)),
            pl.BlockSpec((1, D_MODEL), lambda *_: (0, 0)), resident((D_PLE, D_MODEL)), resident((D_MODEL, D_MODEL))]


def _ffn_half(x, g_pre, g_post, w, layer, tm, ple=None, mix=None):
    m = x.shape[0]
    resident = lambda shape: pl.BlockSpec(shape, lambda i: (0, 0), pipeline_mode=pl.Buffered(1))
    in_specs = [_row_spec(tm, D_MODEL)]
    args = [x]
    tiles_per_seq = 1
    if mix is not None:
        g_mix, ya, bg, cx, conv_w, w_out, seq_len = mix
        tiles_per_seq = seq_len // tm
        per = tm // SUBLANES
        in_specs += [_full_spec((1, D_MODEL)), _row_spec(tm, D_A), _row_spec(tm, D_B), _row_spec(tm, D_B),
                     pl.BlockSpec((SUBLANES, D_B), lambda i: (jnp.maximum(i * per - 1, 0), 0)),
                     _full_spec((CONV_B, D_B)), _full_spec((D_MODEL, D_MODEL))]
        args += [g_mix, ya, bg, cx, cx, conv_w, w_out]
    in_specs += [_full_spec((1, D_MODEL)), _full_spec((1, D_MODEL)),
                 resident((D_MODEL, D_FF)), resident((D_MODEL, D_FF)), resident((D_FF, D_MODEL))]
    args += [g_pre, g_post, *w]
    if ple is not None:
        in_specs += _ple_specs(layer, tm, lambda i: (i, 0))
        args += list(ple)
    return pl.pallas_call(
        functools.partial(_ffn_kernel, with_mix=mix is not None, with_ple=ple is not None,
                          tiles_per_seq=tiles_per_seq),
        grid=(m // tm,),
        in_specs=in_specs,
        out_specs=_row_spec(tm, D_MODEL),
        out_shape=jax.ShapeDtypeStruct((m, D_MODEL), F32),
        compiler_params=_cparams("parallel"),
        name="ffn_half",
    )(*args)


def _ffn_step_kernel(*refs, with_ple):
    refs = list(refs)
    x_ref, gpre_ref, gpost_ref, wig_ref, wiu_ref, wo_ref = refs[:6]
    ple_refs = refs[6:10] if with_ple else []
    o_ref, wigb_ref, wiub_ref, wob_ref, xn_ref, acc_ref = refs[-6:]
    j = pl.program_id(0)

    @pl.when(j == 0)
    def _():
        xn_ref[...] = _rms(x_ref[...], gpre_ref[...]).astype(BF16)
        acc_ref[...] = jnp.zeros_like(acc_ref)

    wig = wig_ref[...].astype(BF16)
    wiu = wiu_ref[...].astype(BF16)
    wo = wo_ref[...].astype(BF16)
    wigb_ref[...] = wig
    wiub_ref[...] = wiu
    wob_ref[...] = wo
    xn = xn_ref[...]
    g = _dot(xn, wig)
    u = _dot(xn, wiu)
    acc_ref[...] += _dot(((g * _sigmoid(g)) * u).astype(BF16), wo)

    @pl.when(j == pl.num_programs(0) - 1)
    def _():
        x = x_ref[...] + 0.5 * _rms(acc_ref[...], gpost_ref[...])
        if with_ple:
            x = _ple(x, *ple_refs)
        o_ref[...] = x


def _ffn_step(x, g_pre, g_post, wi, wo, layer, ple=None):
    m = x.shape[0]
    nf = D_FF // FF_STEP
    const = lambda shape: pl.BlockSpec(shape, lambda j: (0,) * len(shape))
    in_specs = [const((m, D_MODEL)), const((1, D_MODEL)), const((1, D_MODEL)),
                pl.BlockSpec((None, D_MODEL, FF_STEP), lambda j: (layer, 0, j)),
                pl.BlockSpec((None, D_MODEL, FF_STEP), lambda j: (layer, 0, j + nf)),
                pl.BlockSpec((None, FF_STEP, D_MODEL), lambda j: (layer, j, 0))]
    args = [x, g_pre, g_post, wi, wi, wo]
    if ple is not None:
        in_specs += _ple_specs(layer, m, lambda j: (0, 0))
        args += list(ple)
    up_spec = pl.BlockSpec((D_MODEL, FF_STEP), lambda j: (0, j))
    up_shape = jax.ShapeDtypeStruct((D_MODEL, D_FF), BF16)
    out = pl.pallas_call(
        functools.partial(_ffn_step_kernel, with_ple=ple is not None),
        grid=(nf,),
        in_specs=in_specs,
        out_specs=[const((m, D_MODEL)), up_spec, up_spec, pl.BlockSpec((FF_STEP, D_MODEL), lambda j: (j, 0))],
        out_shape=[jax.ShapeDtypeStruct((m, D_MODEL), F32), up_shape, up_shape,
                   jax.ShapeDtypeStruct((D_FF, D_MODEL), BF16)],
        scratch_shapes=[pltpu.VMEM((m, D_MODEL), BF16), pltpu.VMEM((m, D_MODEL), F32)],
        compiler_params=_cparams("arbitrary"),
        name="ffn_step",
    )(*args)
    return out[0], tuple(out[1:])


def _ab_proj_kernel(*refs, q_scale):
    x_ref, g_ref, wa_ref, wkvf_ref, bf_ref = refs[:5]
    q_ref, kt_ref, vt_ref, ktb_ref, vtb_ref, lft_ref, bg_ref, cx_ref = refs[-8:]
    tm = x_ref.shape[0]
    xn = _rms(x_ref[...], g_ref[...]).astype(BF16)
    col = lambda c: _dot(xn, wa_ref[:, c * D_A:(c + 1) * D_A])
    q_ref[...] = (col(0) * q_scale).astype(BF16)
    bg_ref[...] = col(1)
    cx_ref[...] = col(2) * col(3)
    kvf = _dot_nt(wkvf_ref[...], xn)
    kt = kvf[0:D_A]
    vt = kvf[D_A:2 * D_A]
    kt_ref[...] = kt
    vt_ref[...] = vt
    ktb_ref[...] = kt.astype(BF16).reshape(N_PAIR, LANES, tm)
    vtb_ref[:, 0:HD_A, :] = vt.astype(BF16).reshape(H_A, HD_A, tm)
    ones_row = lax.broadcasted_iota(jnp.int32, (H_A, LANES - HD_A, tm), 1) == 0
    vtb_ref[:, HD_A:, :] = jnp.where(ones_row, 1.0, 0.0).astype(BF16)
    lft_ref[...] = _log_sigmoid(kvf[2 * D_A:2 * D_A + H_A] + bf_ref[...])


def _ab_proj(x, g, wa, wkvf, b_f, n_seq, tm, q_scale, layer, n_layers, kv_prev=None):
    m = x.shape[0]
    s_len = m // n_seq
    nt = s_len // tm
    rows = lambda d: pl.BlockSpec((tm, d), lambda b, t: (b * nt + t, 0))
    full = lambda shape: pl.BlockSpec(shape, lambda b, t: (0,) * len(shape))
    kv_spec = pl.BlockSpec((None, None, D_A, tm), lambda b, t: (b, layer, 0, t))
    kv_shape = jax.ShapeDtypeStruct((n_seq, n_layers, D_A, s_len), F32)
    tile_spec = lambda n: pl.BlockSpec((None, n, None, LANES, tm), lambda b, t: (b, 0, t, 0, 0))
    wide = lambda dt: jax.ShapeDtypeStruct((m, D_A), dt)
    tile_shape = lambda n: jax.ShapeDtypeStruct((n_seq, n, nt, LANES, tm), BF16)
    in_specs = [rows(D_MODEL), full((1, D_MODEL)), full((D_MODEL, 4 * D_A)), full((KVF_ROWS, D_MODEL)),
                full((H_A, 1))]
    args = [x, g, wa, wkvf, b_f]
    aliases = {}
    if kv_prev is not None:
        in_specs += [pl.BlockSpec(memory_space=pl.ANY)] * 2
        args += list(kv_prev)
        aliases = {5: 1, 6: 2}
    return pl.pallas_call(
        functools.partial(_ab_proj_kernel, q_scale=q_scale),
        grid=(n_seq, nt),
        in_specs=in_specs,
        out_specs=[rows(D_A), kv_spec, kv_spec, tile_spec(N_PAIR), tile_spec(H_A),
                   pl.BlockSpec((None, H_A, tm), lambda b, t: (b, 0, t)), rows(D_B), rows(D_B)],
        out_shape=[wide(BF16), kv_shape, kv_shape, tile_shape(N_PAIR), tile_shape(H_A),
                   jax.ShapeDtypeStruct((n_seq, H_A, s_len), F32), wide(F32), wide(F32)],
        input_output_aliases=aliases,
        compiler_params=_cparams("parallel", "parallel"),
        name="ab_proj",
    )(*args)


def _cumsum_kernel(x_ref, o_ref):
    x = x_ref[...]
    n = x.shape[1]
    pos = lax.broadcasted_iota(jnp.int32, (1, n), 1)
    d = 1
    while d < n:
        x = x + jnp.where(pos >= d, pltpu.roll(x, d, 1), 0.0)
        d *= 2
    o_ref[...] = x * LOG2E


def _cumsum_lanes(x):
    spec = pl.BlockSpec((None,) + x.shape[1:], lambda b: (b, 0, 0))
    return pl.pallas_call(
        _cumsum_kernel,
        grid=(x.shape[0],),
        in_specs=[spec],
        out_specs=spec,
        out_shape=jax.ShapeDtypeStruct(x.shape, F32),
        compiler_params=_cparams("parallel"),
        name="logf_cumsum",
    )(x)


def _fox_prompt_kernel(q_ref, kt_ref, vt_ref, ck_ref, o_ref):
    qi = pl.program_id(2)
    t = q_ref.shape[0]
    lane = lax.broadcasted_iota(jnp.int32, (1, LANES), 1)
    q = q_ref[...]
    zero = jnp.zeros_like(q)
    qs = (jnp.where(lane < HD_A, q, zero), jnp.where(lane >= HD_A, q, zero))

    def tile(j, state, diagonal):
        kt = kt_ref[j]
        ck = ck_ref[j]
        out = []
        scores = [_dot(qs[e], kt) - ck[e:e + 1, :] for e in range(2)]
        for e in range(2):
            m_prev, acc_prev = state[e]
            s = scores[e]
            if diagonal:
                row = lax.broadcasted_iota(jnp.int32, (t, 1), 0)
                colk = lax.broadcasted_iota(jnp.int32, (1, t), 1)
                s = jnp.where(colk <= row, s, NEG)
            m_new = jnp.maximum(m_prev, jnp.max(s, axis=-1, keepdims=True))
            p = jnp.exp2(s - m_new).astype(BF16)
            acc = jnp.exp2(m_prev - m_new) * acc_prev + _dot_nt(p, vt_ref[e, j])
            out.append((m_new, acc))
        return tuple(out)

    init = tuple((jnp.full((t, 1), NEG, F32), jnp.zeros((t, LANES), F32)) for _ in range(2))
    state = lax.fori_loop(0, qi // 2, lambda i, st: tile(2 * i + 1, tile(2 * i, st, False), False), init)
    (_, a0), (_, a1) = lax.cond(qi % 2 == 1, lambda st: tile(qi, tile(qi - 1, st, False), True),
                                lambda st: tile(qi, st, True), state)
    o0 = a0[:, 0:HD_A] / a0[:, HD_A:HD_A + 1]
    o1 = a1[:, 0:HD_A] / a1[:, HD_A:HD_A + 1]
    o_ref[...] = jnp.concatenate([o0, o1], axis=-1).astype(BF16)


def _fox_prompt(q, ktb, vtb, ck):
    n_seq, _, nb, _, t = ktb.shape
    return pl.pallas_call(
        _fox_prompt_kernel,
        grid=(n_seq, N_PAIR, nb),
        in_specs=[pl.BlockSpec((t, LANES), lambda b, h, i: (b * nb + i, h)),
                  pl.BlockSpec((None, None, nb, LANES, t), lambda b, h, i: (b, h, 0, 0, 0)),
                  pl.BlockSpec((None, 2, nb, LANES, t), lambda b, h, i: (b, h, 0, 0, 0)),
                  pl.BlockSpec((None, None, nb, 2, t), lambda b, h, i: (b, h, 0, 0, 0))],
        out_specs=pl.BlockSpec((t, LANES), lambda b, h, i: (b * nb + i, h)),
        out_shape=jax.ShapeDtypeStruct(q.shape, BF16),
        compiler_params=_cparams("parallel", "parallel", "parallel"),
        name="fox_prompt",
    )(q, ktb, vtb, ck)


def _fox_decode_kernel(pt_ref, q_ref, kn_ref, vn_ref, fn_ref, ft_ref, *refs, n_pages, n_seq):
    k_refs = refs[:n_seq * n_pages]
    v_refs = refs[n_seq * n_pages:2 * n_seq * n_pages]
    o_ref = refs[2 * n_seq * n_pages]
    ps = PAGE_SIZE
    b0 = pl.program_id(0) * n_seq

    lane = lax.broadcasted_iota(jnp.int32, (H_A, D_A), 1)
    hrow = lax.broadcasted_iota(jnp.int32, (H_A, D_A), 0)
    own = (lane // HD_A) == hrow
    pos = lax.broadcasted_iota(jnp.int32, (H_A, ps), 1)

    for sq in range(n_seq):
        qbd_f = jnp.where(own, q_ref[sq].astype(F32), 0.0)
        qbd = qbd_f.astype(BF16)

        carry = fn_ref[sq]
        bias = [None] * n_pages
        for j in reversed(range(n_pages)):
            lf = ft_ref[pt_ref[b0 + sq, j]]
            x = lf
            d = 1
            while d < ps:
                x = x + jnp.where(pos + d < ps, pltpu.roll(x, ps - d, 1), 0.0)
                d *= 2
            bias[j] = (x - lf) + carry
            carry = carry + x[:, 0:1]

        kn = kn_ref[sq].astype(BF16).astype(F32)
        s_new = jnp.sum(qbd_f * kn, axis=-1, keepdims=True)
        scores = []
        m = s_new
        for j in range(n_pages):
            s = _dot(qbd, k_refs[sq * n_pages + j][...].astype(BF16)) + bias[j]
            scores.append(s)
            m = jnp.maximum(m, jnp.max(s, axis=-1, keepdims=True))
        p_new = jnp.exp(s_new - m)
        l = p_new
        vn = vn_ref[sq].astype(BF16).astype(F32)
        acc = p_new.astype(BF16).astype(F32) * vn
        for j in range(n_pages):
            p = jnp.exp(scores[j] - m)
            l = l + jnp.sum(p, axis=-1, keepdims=True)
            acc = acc + _dot_nt(p.astype(BF16), v_refs[sq * n_pages + j][...].astype(BF16))
        o = jnp.where(own, acc / l, 0.0)
        o_ref[sq] = jnp.sum(o, axis=0, keepdims=True).astype(BF16)


def _fox_decode(page_table, q, k_new, v_new, f_new, cache_kt, cache_vt, cache_ft, layer):
    bsz, n_pages = page_table.shape
    n_seq = DECODE_SEQS
    n_phys = cache_ft.shape[0]
    row = lambda dt_shape: pl.BlockSpec((n_seq,) + dt_shape, lambda b, pt: (b, 0, 0))
    kv_specs = [pl.BlockSpec((None, None, D_A, PAGE_SIZE), functools.partial(
        lambda b, pt, sq, j: (pt[b * n_seq + sq, j], layer, 0, 0), sq=sq, j=j))
        for sq in range(n_seq) for j in range(n_pages)]
    f_spec = pl.BlockSpec((n_phys, None, H_A, PAGE_SIZE), lambda b, pt: (0, layer, 0, 0),
                          pipeline_mode=pl.Buffered(1))
    grid_spec = pltpu.PrefetchScalarGridSpec(
        num_scalar_prefetch=1,
        grid=(bsz // n_seq,),
        in_specs=[row((1, D_A)), row((1, D_A)), row((1, D_A)), row((H_A, 1)), f_spec] + kv_specs + kv_specs,
        out_specs=row((1, D_A)),
    )
    n_kv = n_seq * n_pages
    return pl.pallas_call(
        functools.partial(_fox_decode_kernel, n_pages=n_pages, n_seq=n_seq),
        grid_spec=grid_spec,
        out_shape=jax.ShapeDtypeStruct((bsz, 1, D_A), BF16),
        compiler_params=_cparams("parallel"),
        name="fox_decode",
    )(page_table, q, k_new, v_new, f_new, cache_ft, *([cache_kt] * n_kv), *([cache_vt] * n_kv))


def _ab_step_out_kernel(x_ref, g_ref, ya_ref, bg_ref, cx_ref, m1_ref, m2_ref, cw_ref, wo_ref, o_ref):
    o_ref[...] = _short_conv_out(x_ref[...], ya_ref[...], bg_ref[...], cx_ref[...], m1_ref[...], m2_ref[...],
                                 cw_ref, wo_ref, g_ref[...])


def _ab_step_out(x, g, ya, bg, cx, conv_w, w_out, state):
    m = x.shape[0]
    return pl.pallas_call(
        _ab_step_out_kernel,
        grid=(1,),
        in_specs=[_row_spec(m, D_MODEL), _full_spec((1, D_MODEL)), _row_spec(m, D_A), _row_spec(m, D_B),
                  _row_spec(m, D_B), _row_spec(m, D_B), _row_spec(m, D_B), _full_spec((CONV_B, D_B)),
                  _full_spec((D_MODEL, D_MODEL))],
        out_specs=_row_spec(m, D_MODEL),
        out_shape=jax.ShapeDtypeStruct((m, D_MODEL), F32),
        compiler_params=_cparams("arbitrary"),
        name="ab_step_out",
    )(x, g, ya, bg, cx, state[1], state[0], conv_w, w_out)


def _lru_gates(xc, wgi_ref, brg_ref, big_ref, lam_ref):
    r_parts, i_parts = [], []
    for n in range(N_BLK_C):
        ri = _dot(xc[:, n * BS_C:(n + 1) * BS_C].astype(BF16), wgi_ref[n])
        r_parts.append(ri[:, :BS_C])
        i_parts.append(ri[:, BS_C:])
    r = _sigmoid(jnp.concatenate(r_parts, axis=-1) + brg_ref[...])
    ig = _sigmoid(jnp.concatenate(i_parts, axis=-1) + big_ref[...])
    log_a = -LRU_C * r * _softplus(-lam_ref[...])
    a = jnp.exp(log_a)
    th = jnp.tanh(log_a)
    u = jnp.sqrt(-2.0 * th / (1.0 - th)) * (ig * xc)
    return a, u


def _c_mix_kernel(x_ref, g2_ref, g3_ref, win_ref, cw_ref, cb_ref, wgi_ref, brg_ref, big_ref, lam_ref, wout_ref,
                  o_ref, xr_tail_ref, h_tail_ref, prev_ref, h_ref, a_ref, u_ref):
    tm = x_ref.shape[0]
    ngrp = tm // SUBLANES

    @pl.when(pl.program_id(1) == 0)
    def _():
        prev_ref[...] = jnp.zeros_like(prev_ref)
        h_ref[...] = jnp.zeros_like(h_ref)

    x = x_ref[...]
    xn = _rms(x, g2_ref[...]).astype(BF16)
    gate = _dot(xn, win_ref[:, 0:D_RNN])
    xr = _dot(xn, win_ref[:, D_RNN:])
    m1, m2, m3 = _shifted_rows(xr, prev_ref[...], CONV_C - 1)
    xc = m3 * cw_ref[0:1, :] + m2 * cw_ref[1:2, :] + m1 * cw_ref[2:3, :] + xr * cw_ref[3:4, :] + cb_ref[...]
    a, u = _lru_gates(xc, wgi_ref, brg_ref, big_ref, lam_ref)

    a = a.reshape(ngrp, SUBLANES, D_RNN)
    u = u.reshape(ngrp, SUBLANES, D_RNN)
    sub = lax.broadcasted_iota(jnp.int32, (1, SUBLANES, 1), 1)
    d = 1
    while d < SUBLANES:
        ok = sub >= d
        a_sh = jnp.where(ok, pltpu.roll(a, d, 1), 1.0)
        u_sh = jnp.where(ok, pltpu.roll(u, d, 1), 0.0)
        u = a * u_sh + u
        a = a * a_sh
        d *= 2
    a_ref[...] = a
    u_ref[...] = u

    def group(i, h):
        hs = a_ref[i] * h + u_ref[i]
        u_ref[i] = hs
        return hs[SUBLANES - 1:SUBLANES, :]

    h_ref[...] = lax.fori_loop(0, ngrp, group, h_ref[...], unroll=4)
    hs = u_ref[...].reshape(tm, D_RNN)
    z = _gelu_tanh(gate) * hs
    o_ref[...] = x + _rms(_dot(z.astype(BF16), wout_ref[...]), g3_ref[...])
    prev_ref[...] = xr[tm - SUBLANES:, :]
    xr_tail_ref[...] = xr[tm - SUBLANES:, :]
    h_tail_ref[...] = hs[tm - SUBLANES:, :]


def _c_mix(x, g2, g3, w_in, conv_w, conv_b, wgi, b_rg, b_ig, lam, w_out, n_seq, tm):
    m = x.shape[0]
    nt = m // n_seq // tm
    rows = pl.BlockSpec((tm, D_MODEL), lambda b, t: (b * nt + t, 0))
    full = lambda shape: pl.BlockSpec(shape, lambda b, t: (0,) * len(shape))
    tail = pl.BlockSpec((None, SUBLANES, D_RNN), lambda b, t: (b, 0, 0))
    tail_shape = jax.ShapeDtypeStruct((n_seq, SUBLANES, D_RNN), F32)
    return pl.pallas_call(
        _c_mix_kernel,
        grid=(n_seq, nt),
        in_specs=[rows, full((1, D_MODEL)), full((1, D_MODEL)), full((D_MODEL, 2 * D_RNN)), full((CONV_C, D_RNN)),
                  full((1, D_RNN)), full((N_BLK_C, BS_C, 2 * BS_C)), full((1, D_RNN)), full((1, D_RNN)),
                  full((1, D_RNN)), full((D_RNN, D_MODEL))],
        out_specs=[rows, tail, tail],
        out_shape=[jax.ShapeDtypeStruct((m, D_MODEL), F32), tail_shape, tail_shape],
        scratch_shapes=[pltpu.VMEM((SUBLANES, D_RNN), F32), pltpu.VMEM((1, D_RNN), F32),
                        pltpu.VMEM((tm // SUBLANES, SUBLANES, D_RNN), F32),
                        pltpu.VMEM((tm // SUBLANES, SUBLANES, D_RNN), F32)],
        compiler_params=_cparams("parallel", "arbitrary"),
        name="c_mix",
    )(x, g2, g3, w_in, conv_w, conv_b, wgi, b_rg, b_ig, lam, w_out)


def _c_step_kernel(x_ref, g2_ref, g3_ref, win_ref, s0_ref, s1_ref, s2_ref, h0_ref, cw_ref, cb_ref, wgi_ref,
                   brg_ref, big_ref, lam_ref, wout_ref, o_ref, xr_ref, h_ref):
    x = x_ref[...]
    xn = _rms(x, g2_ref[...]).astype(BF16)
    gate = _dot(xn, win_ref[:, 0:D_RNN])
    xr = _dot(xn, win_ref[:, D_RNN:])
    xc = (s0_ref[...] * cw_ref[0:1, :] + s1_ref[...] * cw_ref[1:2, :] + s2_ref[...] * cw_ref[2:3, :]
          + xr * cw_ref[3:4, :] + cb_ref[...])
    a, u = _lru_gates(xc, wgi_ref, brg_ref, big_ref, lam_ref)
    h = a * h0_ref[...] + u
    z = _gelu_tanh(gate) * h
    o_ref[...] = x + _rms(_dot(z.astype(BF16), wout_ref[...]), g3_ref[...])
    xr_ref[...] = xr
    h_ref[...] = h


def _c_step(x, g2, g3, w_in, state, h0, conv_w, conv_b, wgi, b_rg, b_ig, lam, w_out):
    m = x.shape[0]
    rows = _row_spec(m, D_RNN)
    shape = jax.ShapeDtypeStruct((m, D_RNN), F32)
    return pl.pallas_call(
        _c_step_kernel,
        grid=(1,),
        in_specs=[rows, _full_spec((1, D_MODEL)), _full_spec((1, D_MODEL)), _full_spec((D_MODEL, 2 * D_RNN)),
                  rows, rows, rows, rows, _full_spec((CONV_C, D_RNN)), _full_spec((1, D_RNN)),
                  _full_spec((N_BLK_C, BS_C, 2 * BS_C)), _full_spec((1, D_RNN)), _full_spec((1, D_RNN)),
                  _full_spec((1, D_RNN)), _full_spec((D_RNN, D_MODEL))],
        out_specs=[rows, rows, rows],
        out_shape=[shape, shape, shape],
        compiler_params=_cparams("arbitrary"),
        name="c_step",
    )(x, g2, g3, w_in, state[0], state[1], state[2], h0, conv_w, conv_b, wgi, b_rg, b_ig, lam, w_out)


def kernel(x_prompt, x_sample, p_prompt, p_sample, cache_k, cache_v, cache_logf, state_conv_b, state_conv_c, state_h_c, page_table, norms, ffn1_wi, ffn1_wo, ffn2_wi, ffn2_wo, ple_w_pe, ple_w_pg, ab_w_in, ab_b_f, ab_conv_w, ab_w_out, c_w_in, c_conv_w, c_conv_b, c_w_rg, c_b_rg, c_w_ig, c_b_ig, c_lam, c_w_out):
    bp, s_len, _ = x_prompt.shape
    bs = x_sample.shape[0]
    depth = norms.shape[0]
    n_phys, n_ab = cache_k.shape[0], cache_k.shape[1]
    mp = bp * s_len
    tm_x = min(512, s_len)

    xp = x_prompt.reshape(mp, D_MODEL)
    xs = x_sample.reshape(bs, D_MODEL)
    pp = p_prompt.reshape(depth, mp, D_PLE)
    psm = p_sample.reshape(depth, bs, D_PLE)
    cache_kt = jnp.transpose(cache_k, (0, 1, 3, 4, 2)).reshape(n_phys, n_ab, D_A, PAGE_SIZE)
    cache_vt = jnp.transpose(cache_v, (0, 1, 3, 4, 2)).reshape(n_phys, n_ab, D_A, PAGE_SIZE)
    cache_ft = jnp.swapaxes(cache_logf, 2, 3)
    w_pe = ple_w_pe.astype(BF16)
    w_pg = ple_w_pg.astype(BF16)

    kv_p = kv_s = None
    fp_l, fs_l = [], []
    cbp_l, cbs_l, ccp_l, ccs_l, hp_l, hs_l = [], [], [], [], [], []
    for i in range(depth):
        g = norms[i].reshape(norms.shape[1], 1, D_MODEL)
        xs, w_ffn = _ffn_step(xs, g[0], g[1], ffn1_wi, ffn1_wo, i)
        xp = _ffn_half(xp, g[0], g[1], w_ffn, i, 2 * tm_x)
        l = i // 2
        if i % 2 == 0:
            w_in = ab_w_in[l]
            wa = jnp.concatenate([w_in[:, :D_A], w_in[:, 3 * D_A + H_A:]], axis=1).astype(BF16)
            wkvf = jnp.pad(w_in[:, D_A:3 * D_A + H_A].T, ((0, BF16_ROWS - H_A), (0, 0))).astype(BF16)
            b_f = ab_b_f[l].reshape(H_A, 1)
            w_out = ab_w_out[l].astype(BF16)
            q, kt, vt, ktb, vtb, lft, bg, cx = _ab_proj(xp, g[2], wa, wkvf, b_f, bp, tm_x, HD_A ** -0.5 * LOG2E,
                                                        l, n_ab, kv_p)
            kv_p = (kt, vt)
            nb = s_len // tm_x
            ck = _cumsum_lanes(lft).reshape(bp, N_PAIR, 2, nb, tm_x).transpose(0, 1, 3, 2, 4)
            ya = _fox_prompt(q, ktb, vtb, ck)
            mix = (g[3], ya, bg, cx, ab_conv_w[l], w_out, s_len)
            fp_l.append(lft)
            cbp_l.append(cx.reshape(bp, s_len, D_B)[:, s_len - (CONV_B - 1):])
            q, kt, vt, _, _, lft, bg, cx = _ab_proj(xs, g[2], wa, wkvf, b_f, 1, bs, HD_A ** -0.5, l, n_ab, kv_s)
            kv_s = (kt, vt)
            ya = _fox_decode(page_table, q.reshape(bs, 1, D_A), kt[0, l].T.reshape(bs, 1, D_A),
                             vt[0, l].T.reshape(bs, 1, D_A), lft[0].T.reshape(bs, H_A, 1),
                             cache_kt, cache_vt, cache_ft, l)
            st = state_conv_b[:, l]
            xs = _ab_step_out(xs, g[3], ya.reshape(bs, D_A), bg, cx, ab_conv_w[l], w_out, (st[:, 0], st[:, 1]))
            fs_l.append(lft[0])
            cbs_l.append(jnp.stack([st[:, 1], cx], axis=1))
        else:
            w_in = c_w_in[l].astype(BF16)
            wgi = jnp.concatenate([c_w_rg[l], c_w_ig[l]], axis=-1).astype(BF16)
            row = lambda a: a.reshape(1, D_RNN)
            gate_args = (c_conv_w[l], row(c_conv_b[l]), wgi, row(c_b_rg[l]), row(c_b_ig[l]), row(c_lam[l]),
                         c_w_out[l].astype(BF16))
            xp, xr_tail, h_tail = _c_mix(xp, g[2], g[3], w_in, *gate_args, bp, tm_x)
            mix = None
            ccp_l.append(xr_tail[:, SUBLANES - (CONV_C - 1):])
            hp_l.append(h_tail[:, SUBLANES - 1])
            st = state_conv_c[:, l]
            xs, xr, h = _c_step(xs, g[2], g[3], w_in, (st[:, 0], st[:, 1], st[:, 2]), state_h_c[:, l], *gate_args)
            ccs_l.append(jnp.stack([st[:, 1], st[:, 2], xr], axis=1))
            hs_l.append(h)
        xs, w_ffn = _ffn_step(xs, g[4], g[5], ffn2_wi, ffn2_wo, i, ple=(psm, g[6], w_pe, w_pg))
        xp = _ffn_half(xp, g[4], g[5], w_ffn, i, tm_x, ple=(pp, g[6], w_pe, w_pg), mix=mix)

    st = lambda lst: jnp.stack(lst, axis=1)
    out_p = lambda a: a.reshape(bp, n_ab, H_A, HD_A, s_len).transpose(0, 1, 4, 2, 3)
    out_s = lambda a: a.reshape(n_ab, 1, H_A, HD_A, bs).transpose(4, 0, 1, 2, 3)
    return (xp.reshape(bp, s_len, D_MODEL), xs.reshape(bs, 1, D_MODEL),
            out_p(kv_p[0]), out_p(kv_p[1]), st(fp_l).transpose(0, 1, 3, 2),
            out_s(kv_s[0]), out_s(kv_s[1]), jnp.stack(fs_l, axis=0).reshape(n_ab, 1, H_A, bs).transpose(3, 0, 1, 2),
            st(cbp_l), st(cbs_l), st(ccp_l), st(ccs_l), st(hp_l), st(hs_l))
```

```python
import functools

import jax
import jax.numpy as jnp
from jax import lax
from jax.experimental import pallas as pl
from jax.experimental.pallas import tpu as pltpu

F32 = jnp.float32
BF16 = jnp.bfloat16

D_MODEL = 1024
D_PLE = 256
H_A = 8
HD_A = 64
D_A = H_A * HD_A
D_B = D_MODEL - D_A
CONV_B = 3
D_RNN = D_MODEL
N_BLK_C = 8
BS_C = D_RNN // N_BLK_C
CONV_C = 4
LRU_C = 8.0
D_FF = 2816
EPS = 1e-6
NEG = -1e30
PAGE_SIZE = 128

LANES = 128
SUBLANES = 8
BF16_ROWS = 16
FF_CHUNKS = ((0, 1024), (1024, 1024), (2048, 768))
FF_STEP = 256
FFN_ROWS = 256
C_ROWS = 256
LOG2E = 1.4426950408889634
DECODE_SEQS = 2
N_PAIR = D_A // LANES
KVF_ROWS = 2 * D_A + BF16_ROWS
VMEM_LIMIT = 56 * 1024 * 1024

_NT = (((1,), (1,)), ((), ()))


def _cparams(*sem):
    return pltpu.CompilerParams(dimension_semantics=sem, vmem_limit_bytes=VMEM_LIMIT)


def _rms(x, g):
    return x * lax.rsqrt(jnp.mean(x * x, axis=-1, keepdims=True) + EPS) * g


def _dot(a, b):
    return jnp.dot(a, b, preferred_element_type=F32)


def _dot_nt(a, b):
    return lax.dot_general(a, b, _NT, preferred_element_type=F32)


def _log_sigmoid(x):
    return jnp.minimum(x, 0.0) - jnp.log1p(jnp.exp(-jnp.abs(x)))


def _softplus(x):
    return jnp.maximum(x, 0.0) + jnp.log1p(jnp.exp(-jnp.abs(x)))


def _sigmoid(x):
    return 1.0 / (1.0 + jnp.exp(-x))


def _gelu_tanh(x):
    c = 0.7978845608028654
    hx = 0.5 * x
    return hx + hx * jnp.tanh(x * (c + (0.044715 * c) * (x * x)))


def _row_spec(tm, d):
    return pl.BlockSpec((tm, d), lambda i: (i, 0))


def _full_spec(shape):
    nd = len(shape)
    return pl.BlockSpec(shape, lambda i: (0,) * nd)


def _shifted_rows(cur, prev, n_shift):
    sub = lax.broadcasted_iota(jnp.int32, (SUBLANES, 1), 0)
    out = []
    for sft in range(1, n_shift + 1):
        r = pltpu.roll(cur, sft, 0)
        top = r[0:SUBLANES, :]
        for i in range(sft):
            top = jnp.where(sub == i, prev[SUBLANES - sft + i:SUBLANES - sft + i + 1, :], top)
        out.append(jnp.concatenate([top, r[SUBLANES:, :]], axis=0))
    return out


def _short_conv_out(x, ya, bg, cx, m1, m2, cw_ref, wo_ref, g):
    conv = m2 * cw_ref[0:1, :] + m1 * cw_ref[1:2, :] + cx * cw_ref[2:3, :]
    yb = bg * conv
    y = _dot(ya, wo_ref[0:D_A, :]) + _dot(yb.astype(BF16), wo_ref[D_A:, :])
    return x + _rms(y, g)


def _ffn_kernel(*refs, with_mix, with_ple, tiles_per_seq):
    refs = list(refs)
    x = refs.pop(0)[...]
    if with_mix:
        gmix_ref, ya_ref, bg_ref, cx_ref, prev_ref, cw_ref, wmix_ref = refs[:7]
        del refs[:7]
        cx = cx_ref[...]
        first = (pl.program_id(0) % tiles_per_seq) == 0
        m1, m2 = _shifted_rows(cx, prev_ref[...] * jnp.where(first, 0.0, 1.0), CONV_B - 1)
        x = _short_conv_out(x, ya_ref[...], bg_ref[...], cx, m1, m2, cw_ref, wmix_ref, gmix_ref[...])
    gpre_ref, gpost_ref, wig_ref, wiu_ref, wo_ref = refs[:5]
    del refs[:5]
    n_grp = max(1, x.shape[0] // FFN_ROWS)
    xg = [x[r * FFN_ROWS:(r + 1) * FFN_ROWS] for r in range(n_grp)] if n_grp > 1 else [x]
    xn = [_rms(xr, gpre_ref[...]).astype(BF16) for xr in xg]
    acc = [None] * n_grp
    for c0, cw in FF_CHUNKS:
        for r in range(n_grp):
            g = _dot(xn[r], wig_ref[:, c0:c0 + cw])
            u = _dot(xn[r], wiu_ref[:, c0:c0 + cw])
            h = (g * _sigmoid(g)) * u
            part = _dot(h.astype(BF16), wo_ref[c0:c0 + cw, :])
            acc[r] = part if acc[r] is None else acc[r] + part
    xg = [xr + 0.5 * _rms(a, gpost_ref[...]) for xr, a in zip(xg, acc)]
    x = jnp.concatenate(xg, axis=0) if n_grp > 1 else xg[0]
    if with_ple:
        x = _ple(x, *refs[:4])
        del refs[:4]
    o_ref, = refs
    o_ref[...] = x


def _ple(x, p_ref, g_ref, wpe_ref, wpg_ref):
    gate = _sigmoid(_dot(_rms(x, g_ref[...]).astype(BF16), wpg_ref[...]))
    return x + _dot(p_ref[...].astype(BF16), wpe_ref[...]) * gate


def _ple_specs(layer, tm, index):
    resident = lambda shape: pl.BlockSpec((None,) + shape, lambda *_: (layer, 0, 0), pipeline_mode=pl.Buffered(1))
    return [pl.BlockSpec((None, tm, D_PLE), lambda *i: (layer,) + index(*i)),
            pl.BlockSpec((1, D_MODEL), lambda *_: (0, 0)), resident((D_PLE, D_MODEL)), resident((D_MODEL, D_MODEL))]


def _ffn_half(x, g_pre, g_post, w, layer, tm, ple=None, mix=None):
    m = x.shape[0]
    resident = lambda shape: pl.BlockSpec(shape, lambda i: (0, 0), pipeline_mode=pl.Buffered(1))
    in_specs = [_row_spec(tm, D_MODEL)]
    args = [x]
    tiles_per_seq = 1
    if mix is not None:
        g_mix, ya, bg, cx, conv_w, w_out, seq_len = mix
        tiles_per_seq = seq_len // tm
        per = tm // SUBLANES
        in_specs += [_full_spec((1, D_MODEL)), _row_spec(tm, D_A), _row_spec(tm, D_B), _row_spec(tm, D_B),
                     pl.BlockSpec((SUBLANES, D_B), lambda i: (jnp.maximum(i * per - 1, 0), 0)),
                     _full_spec((CONV_B, D_B)), _full_spec((D_MODEL, D_MODEL))]
        args += [g_mix, ya, bg, cx, cx, conv_w, w_out]
    in_specs += [_full_spec((1, D_MODEL)), _full_spec((1, D_MODEL)),
                 resident((D_MODEL, D_FF)), resident((D_MODEL, D_FF)), resident((D_FF, D_MODEL))]
    args += [g_pre, g_post, *w]
    if ple is not None:
        in_specs += _ple_specs(layer, tm, lambda i: (i, 0))
        args += list(ple)
    return pl.pallas_call(
        functools.partial(_ffn_kernel, with_mix=mix is not None, with_ple=ple is not None,
                          tiles_per_seq=tiles_per_seq),
        grid=(m // tm,),
        in_specs=in_specs,
        out_specs=_row_spec(tm, D_MODEL),
        out_shape=jax.ShapeDtypeStruct((m, D_MODEL), F32),
        compiler_params=_cparams("parallel"),
        name="ffn_half",
    )(*args)


def _ffn_step_kernel(*refs, with_ple):
    refs = list(refs)
    x_ref, gpre_ref, gpost_ref, wig_ref, wiu_ref, wo_ref = refs[:6]
    ple_refs = refs[6:10] if with_ple else []
    o_ref, wigb_ref, wiub_ref, wob_ref, xn_ref, acc_ref = refs[-6:]
    j = pl.program_id(0)

    @pl.when(j == 0)
    def _():
        xn_ref[...] = _rms(x_ref[...], gpre_ref[...]).astype(BF16)
        acc_ref[...] = jnp.zeros_like(acc_ref)

    wig = wig_ref[...].astype(BF16)
    wiu = wiu_ref[...].astype(BF16)
    wo = wo_ref[...].astype(BF16)
    wigb_ref[...] = wig
    wiub_ref[...] = wiu
    wob_ref[...] = wo
    xn = xn_ref[...]
    g = _dot(xn, wig)
    u = _dot(xn, wiu)
    acc_ref[...] += _dot(((g * _sigmoid(g)) * u).astype(BF16), wo)

    @pl.when(j == pl.num_programs(0) - 1)
    def _():
        x = x_ref[...] + 0.5 * _rms(acc_ref[...], gpost_ref[...])
        if with_ple:
            x = _ple(x, *ple_refs)
        o_ref[...] = x


def _ffn_step(x, g_pre, g_post, wi, wo, layer, ple=None):
    m = x.shape[0]
    nf = D_FF // FF_STEP
    const = lambda shape: pl.BlockSpec(shape, lambda j: (0,) * len(shape))
    in_specs = [const((m, D_MODEL)), const((1, D_MODEL)), const((1, D_MODEL)),
                pl.BlockSpec((None, D_MODEL, FF_STEP), lambda j: (layer, 0, j)),
                pl.BlockSpec((None, D_MODEL, FF_STEP), lambda j: (layer, 0, j + nf)),
                pl.BlockSpec((None, FF_STEP, D_MODEL), lambda j: (layer, j, 0))]
    args = [x, g_pre, g_post, wi, wi, wo]
    if ple is not None:
        in_specs += _ple_specs(layer, m, lambda j: (0, 0))
        args += list(ple)
    up_spec = pl.BlockSpec((D_MODEL, FF_STEP), lambda j: (0, j))
    up_shape = jax.ShapeDtypeStruct((D_MODEL, D_FF), BF16)
    out = pl.pallas_call(
        functools.partial(_ffn_step_kernel, with_ple=ple is not None),
        grid=(nf,),
        in_specs=in_specs,
        out_specs=[const((m, D_MODEL)), up_spec, up_spec, pl.BlockSpec((FF_STEP, D_MODEL), lambda j: (j, 0))],
        out_shape=[jax.ShapeDtypeStruct((m, D_MODEL), F32), up_shape, up_shape,
                   jax.ShapeDtypeStruct((D_FF, D_MODEL), BF16)],
        scratch_shapes=[pltpu.VMEM((m, D_MODEL), BF16), pltpu.VMEM((m, D_MODEL), F32)],
        compiler_params=_cparams("arbitrary"),
        name="ffn_step",
    )(*args)
    return out[0], tuple(out[1:])


def _ab_proj_kernel(*refs, q_scale):
    x_ref, g_ref, wa_ref, wkvf_ref, bf_ref = refs[:5]
    q_ref, kt_ref, vt_ref, ktb_ref, vtb_ref, lft_ref, bg_ref, cx_ref = refs[-8:]
    tm = x_ref.shape[0]
    xn = _rms(x_ref[...], g_ref[...]).astype(BF16)
    col = lambda c: _dot(xn, wa_ref[:, c * D_A:(c + 1) * D_A])
    q_ref[...] = (col(0) * q_scale).astype(BF16)
    bg_ref[...] = col(1)
    cx_ref[...] = col(2) * col(3)
    kvf = _dot_nt(wkvf_ref[...], xn)
    kt = kvf[0:D_A]
    vt = kvf[D_A:2 * D_A]
    kt_ref[...] = kt
    vt_ref[...] = vt
    ktb_ref[...] = kt.astype(BF16).reshape(N_PAIR, LANES, tm)
    vtb_ref[:, 0:HD_A, :] = vt.astype(BF16).reshape(H_A, HD_A, tm)
    ones_row = lax.broadcasted_iota(jnp.int32, (H_A, LANES - HD_A, tm), 1) == 0
    vtb_ref[:, HD_A:, :] = jnp.where(ones_row, 1.0, 0.0).astype(BF16)
    lft_ref[...] = _log_sigmoid(kvf[2 * D_A:2 * D_A + H_A] + bf_ref[...])


def _ab_proj(x, g, wa, wkvf, b_f, n_seq, tm, q_scale, layer, n_layers, kv_prev=None):
    m = x.shape[0]
    s_len = m // n_seq
    nt = s_len // tm
    rows = lambda d: pl.BlockSpec((tm, d), lambda b, t: (b * nt + t, 0))
    full = lambda shape: pl.BlockSpec(shape, lambda b, t: (0,) * len(shape))
    kv_spec = pl.BlockSpec((None, None, D_A, tm), lambda b, t: (b, layer, 0, t))
    kv_shape = jax.ShapeDtypeStruct((n_seq, n_layers, D_A, s_len), F32)
    tile_spec = lambda n: pl.BlockSpec((None, n, None, LANES, tm), lambda b, t: (b, 0, t, 0, 0))
    wide = lambda dt: jax.ShapeDtypeStruct((m, D_A), dt)
    tile_shape = lambda n: jax.ShapeDtypeStruct((n_seq, n, nt, LANES, tm), BF16)
    in_specs = [rows(D_MODEL), full((1, D_MODEL)), full((D_MODEL, 4 * D_A)), full((KVF_ROWS, D_MODEL)),
                full((H_A, 1))]
    args = [x, g, wa, wkvf, b_f]
    aliases = {}
    if kv_prev is not None:
        in_specs += [pl.BlockSpec(memory_space=pl.ANY)] * 2
        args += list(kv_prev)
        aliases = {5: 1, 6: 2}
    return pl.pallas_call(
        functools.partial(_ab_proj_kernel, q_scale=q_scale),
        grid=(n_seq, nt),
        in_specs=in_specs,
        out_specs=[rows(D_A), kv_spec, kv_spec, tile_spec(N_PAIR), tile_spec(H_A),
                   pl.BlockSpec((None, H_A, tm), lambda b, t: (b, 0, t)), rows(D_B), rows(D_B)],
        out_shape=[wide(BF16), kv_shape, kv_shape, tile_shape(N_PAIR), tile_shape(H_A),
                   jax.ShapeDtypeStruct((n_seq, H_A, s_len), F32), wide(F32), wide(F32)],
        input_output_aliases=aliases,
        compiler_params=_cparams("parallel", "parallel"),
        name="ab_proj",
    )(*args)


def _cumsum_kernel(x_ref, o_ref):
    x = x_ref[...]
    n = x.shape[1]
    pos = lax.broadcasted_iota(jnp.int32, (1, n), 1)
    d = 1
    while d < n:
        x = x + jnp.where(pos >= d, pltpu.roll(x, d, 1), 0.0)
        d *= 2
    o_ref[...] = x * LOG2E


def _cumsum_lanes(x):
    spec = pl.BlockSpec((None,) + x.shape[1:], lambda b: (b, 0, 0))
    return pl.pallas_call(
        _cumsum_kernel,
        grid=(x.shape[0],),
        in_specs=[spec],
        out_specs=spec,
        out_shape=jax.ShapeDtypeStruct(x.shape, F32),
        compiler_params=_cparams("parallel"),
        name="logf_cumsum",
    )(x)


def _fox_prompt_kernel(q_ref, kt_ref, vt_ref, ck_ref, o_ref):
    qi = pl.program_id(2)
    t = q_ref.shape[0]
    lane = lax.broadcasted_iota(jnp.int32, (1, LANES), 1)
    q = q_ref[...]
    zero = jnp.zeros_like(q)
    qs = (jnp.where(lane < HD_A, q, zero), jnp.where(lane >= HD_A, q, zero))

    def tile(j, state, diagonal):
        kt = kt_ref[j]
        ck = ck_ref[j]
        out = []
        scores = [_dot(qs[e], kt) - ck[e:e + 1, :] for e in range(2)]
        for e in range(2):
            m_prev, acc_prev = state[e]
            s = scores[e]
            if diagonal:
                row = lax.broadcasted_iota(jnp.int32, (t, 1), 0)
                colk = lax.broadcasted_iota(jnp.int32, (1, t), 1)
                s = jnp.where(colk <= row, s, NEG)
            m_new = jnp.maximum(m_prev, jnp.max(s, axis=-1, keepdims=True))
            p = jnp.exp2(s - m_new).astype(BF16)
            acc = jnp.exp2(m_prev - m_new) * acc_prev + _dot_nt(p, vt_ref[e, j])
            out.append((m_new, acc))
        return tuple(out)

    init = tuple((jnp.full((t, 1), NEG, F32), jnp.zeros((t, LANES), F32)) for _ in range(2))
    state = lax.fori_loop(0, qi // 2, lambda i, st: tile(2 * i + 1, tile(2 * i, st, False), False), init)
    (_, a0), (_, a1) = lax.cond(qi % 2 == 1, lambda st: tile(qi, tile(qi - 1, st, False), True),
                                lambda st: tile(qi, st, True), state)
    o0 = a0[:, 0:HD_A] / a0[:, HD_A:HD_A + 1]
    o1 = a1[:, 0:HD_A] / a1[:, HD_A:HD_A + 1]
    o_ref[...] = jnp.concatenate([o0, o1], axis=-1).astype(BF16)


def _fox_prompt(q, ktb, vtb, ck):
    n_seq, _, nb, _, t = ktb.shape
    return pl.pallas_call(
        _fox_prompt_kernel,
        grid=(n_seq, N_PAIR, nb),
        in_specs=[pl.BlockSpec((t, LANES), lambda b, h, i: (b * nb + i, h)),
                  pl.BlockSpec((None, None, nb, LANES, t), lambda b, h, i: (b, h, 0, 0, 0)),
                  pl.BlockSpec((None, 2, nb, LANES, t), lambda b, h, i: (b, h, 0, 0, 0)),
                  pl.BlockSpec((None, None, nb, 2, t), lambda b, h, i: (b, h, 0, 0, 0))],
        out_specs=pl.BlockSpec((t, LANES), lambda b, h, i: (b * nb + i, h)),
        out_shape=jax.ShapeDtypeStruct(q.shape, BF16),
        compiler_params=_cparams("parallel", "parallel", "parallel"),
        name="fox_prompt",
    )(q, ktb, vtb, ck)


def _fox_decode_kernel(pt_ref, q_ref, kn_ref, vn_ref, fn_ref, ft_ref, *refs, n_pages, n_seq):
    k_refs = refs[:n_seq * n_pages]
    v_refs = refs[n_seq * n_pages:2 * n_seq * n_pages]
    o_ref = refs[2 * n_seq * n_pages]
    ps = PAGE_SIZE
    b0 = pl.program_id(0) * n_seq

    lane = lax.broadcasted_iota(jnp.int32, (H_A, D_A), 1)
    hrow = lax.broadcasted_iota(jnp.int32, (H_A, D_A), 0)
    own = (lane // HD_A) == hrow
    pos = lax.broadcasted_iota(jnp.int32, (H_A, ps), 1)

    for sq in range(n_seq):
        qbd_f = jnp.where(own, q_ref[sq].astype(F32), 0.0)
        qbd = qbd_f.astype(BF16)

        carry = fn_ref[sq]
        bias = [None] * n_pages
        for j in reversed(range(n_pages)):
            lf = ft_ref[pt_ref[b0 + sq, j]]
            x = lf
            d = 1
            while d < ps:
                x = x + jnp.where(pos + d < ps, pltpu.roll(x, ps - d, 1), 0.0)
                d *= 2
            bias[j] = (x - lf) + carry
            carry = carry + x[:, 0:1]

        kn = kn_ref[sq].astype(BF16).astype(F32)
        s_new = jnp.sum(qbd_f * kn, axis=-1, keepdims=True)
        scores = []
        m = s_new
        for j in range(n_pages):
            s = _dot(qbd, k_refs[sq * n_pages + j][...].astype(BF16)) + bias[j]
            scores.append(s)
            m = jnp.maximum(m, jnp.max(s, axis=-1, keepdims=True))
        p_new = jnp.exp(s_new - m)
        l = p_new
        vn = vn_ref[sq].astype(BF16).astype(F32)
        acc = p_new.astype(BF16).astype(F32) * vn
        for j in range(n_pages):
            p = jnp.exp(scores[j] - m)
            l = l + jnp.sum(p, axis=-1, keepdims=True)
            acc = acc + _dot_nt(p.astype(BF16), v_refs[sq * n_pages + j][...].astype(BF16))
        o = jnp.where(own, acc / l, 0.0)
        o_ref[sq] = jnp.sum(o, axis=0, keepdims=True).astype(BF16)


def _fox_decode(page_table, q, k_new, v_new, f_new, cache_kt, cache_vt, cache_ft, layer):
    bsz, n_pages = page_table.shape
    n_seq = DECODE_SEQS
    n_phys = cache_ft.shape[0]
    row = lambda dt_shape: pl.BlockSpec((n_seq,) + dt_shape, lambda b, pt: (b, 0, 0))
    kv_specs = [pl.BlockSpec((None, None, D_A, PAGE_SIZE), functools.partial(
        lambda b, pt, sq, j: (pt[b * n_seq + sq, j], layer, 0, 0), sq=sq, j=j))
        for sq in range(n_seq) for j in range(n_pages)]
    f_spec = pl.BlockSpec((n_phys, None, H_A, PAGE_SIZE), lambda b, pt: (0, layer, 0, 0),
                          pipeline_mode=pl.Buffered(1))
    grid_spec = pltpu.PrefetchScalarGridSpec(
        num_scalar_prefetch=1,
        grid=(bsz // n_seq,),
        in_specs=[row((1, D_A)), row((1, D_A)), row((1, D_A)), row((H_A, 1)), f_spec] + kv_specs + kv_specs,
        out_specs=row((1, D_A)),
    )
    n_kv = n_seq * n_pages
    return pl.pallas_call(
        functools.partial(_fox_decode_kernel, n_pages=n_pages, n_seq=n_seq),
        grid_spec=grid_spec,
        out_shape=jax.ShapeDtypeStruct((bsz, 1, D_A), BF16),
        compiler_params=_cparams("parallel"),
        name="fox_decode",
    )(page_table, q, k_new, v_new, f_new, cache_ft, *([cache_kt] * n_kv), *([cache_vt] * n_kv))


def _ab_step_out_kernel(x_ref, g_ref, ya_ref, bg_ref, cx_ref, m1_ref, m2_ref, cw_ref, wo_ref, o_ref):
    o_ref[...] = _short_conv_out(x_ref[...], ya_ref[...], bg_ref[...], cx_ref[...], m1_ref[...], m2_ref[...],
                                 cw_ref, wo_ref, g_ref[...])


def _ab_step_out(x, g, ya, bg, cx, conv_w, w_out, state):
    m = x.shape[0]
    return pl.pallas_call(
        _ab_step_out_kernel,
        grid=(1,),
        in_specs=[_row_spec(m, D_MODEL), _full_spec((1, D_MODEL)), _row_spec(m, D_A), _row_spec(m, D_B),
                  _row_spec(m, D_B), _row_spec(m, D_B), _row_spec(m, D_B), _full_spec((CONV_B, D_B)),
                  _full_spec((D_MODEL, D_MODEL))],
        out_specs=_row_spec(m, D_MODEL),
        out_shape=jax.ShapeDtypeStruct((m, D_MODEL), F32),
        compiler_params=_cparams("arbitrary"),
        name="ab_step_out",
    )(x, g, ya, bg, cx, state[1], state[0], conv_w, w_out)


def _lru_gates(xc, wgi_ref, brg_ref, big_ref, lam_ref):
    r_parts, i_parts = [], []
    for n in range(N_BLK_C):
        ri = _dot(xc[:, n * BS_C:(n + 1) * BS_C].astype(BF16), wgi_ref[n])
        r_parts.append(ri[:, :BS_C])
        i_parts.append(ri[:, BS_C:])
    r = _sigmoid(jnp.concatenate(r_parts, axis=-1) + brg_ref[...])
    ig = _sigmoid(jnp.concatenate(i_parts, axis=-1) + big_ref[...])
    log_a = r * (-LRU_C * _softplus(-lam_ref[...]))
    a = jnp.exp(log_a)
    th = jnp.tanh(log_a)
    u = jnp.sqrt(-2.0 * th / (1.0 - th)) * (ig * xc)
    return a, u


def _c_mix_kernel(x_ref, g2_ref, g3_ref, win_ref, cw_ref, cb_ref, wgi_ref, brg_ref, big_ref, lam_ref, wout_ref,
                  o_ref, xr_tail_ref, h_tail_ref, prev_ref, h_ref):
    tm = x_ref.shape[0]
    rows = min(C_ROWS, tm)
    ngrp = rows // SUBLANES

    @pl.when(pl.program_id(1) == 0)
    def _():
        prev_ref[...] = jnp.zeros_like(prev_ref)
        h_ref[...] = jnp.zeros_like(h_ref)

    prev = prev_ref[...]
    h = h_ref[...]
    sub = lax.broadcasted_iota(jnp.int32, (1, SUBLANES, 1), 1)
    outs = []
    proj = []
    for r0 in range(0, tm, rows):
        x = x_ref[r0:r0 + rows, :]
        xn = _rms(x, g2_ref[...]).astype(BF16)
        proj.append((x, _dot(xn, win_ref[:, 0:D_RNN]), _dot(xn, win_ref[:, D_RNN:])))
    for x, gate, xr in proj:
        m1, m2, m3 = _shifted_rows(xr, prev, CONV_C - 1)
        prev = xr[rows - SUBLANES:, :]
        xc = m3 * cw_ref[0:1, :] + m2 * cw_ref[1:2, :] + m1 * cw_ref[2:3, :] + xr * cw_ref[3:4, :] + cb_ref[...]
        a, u = _lru_gates(xc, wgi_ref, brg_ref, big_ref, lam_ref)

        a = a.reshape(ngrp, SUBLANES, D_RNN)
        u = u.reshape(ngrp, SUBLANES, D_RNN)
        d = 1
        while d < SUBLANES:
            ok = sub >= d
            a_sh = jnp.where(ok, pltpu.roll(a, d, 1), 1.0)
            u_sh = jnp.where(ok, pltpu.roll(u, d, 1), 0.0)
            u = a * u_sh + u
            a = a * a_sh
            d *= 2
        groups = []
        for i in range(ngrp):
            hs_i = a[i] * h + u[i]
            h = hs_i[SUBLANES - 1:SUBLANES, :]
            groups.append(hs_i)
        hs = jnp.concatenate(groups, axis=0)
        z = _gelu_tanh(gate) * hs
        outs.append(x + _rms(_dot(z.astype(BF16), wout_ref[...]), g3_ref[...]))
    o_ref[...] = jnp.concatenate(outs, axis=0)
    prev_ref[...] = prev
    h_ref[...] = h
    xr_tail_ref[...] = prev
    h_tail_ref[...] = hs[rows - SUBLANES:, :]


def _c_mix(x, g2, g3, w_in, conv_w, conv_b, wgi, b_rg, b_ig, lam, w_out, n_seq, tm):
    m = x.shape[0]
    nt = m // n_seq // tm
    rows = pl.BlockSpec((tm, D_MODEL), lambda b, t: (b * nt + t, 0))
    full = lambda shape: pl.BlockSpec(shape, lambda b, t: (0,) * len(shape))
    tail = pl.BlockSpec((None, SUBLANES, D_RNN), lambda b, t: (b, 0, 0))
    tail_shape = jax.ShapeDtypeStruct((n_seq, SUBLANES, D_RNN), F32)
    return pl.pallas_call(
        _c_mix_kernel,
        grid=(n_seq, nt),
        in_specs=[rows, full((1, D_MODEL)), full((1, D_MODEL)), full((D_MODEL, 2 * D_RNN)), full((CONV_C, D_RNN)),
                  full((1, D_RNN)), full((N_BLK_C, BS_C, 2 * BS_C)), full((1, D_RNN)), full((1, D_RNN)),
                  full((1, D_RNN)), full((D_RNN, D_MODEL))],
        out_specs=[rows, tail, tail],
        out_shape=[jax.ShapeDtypeStruct((m, D_MODEL), F32), tail_shape, tail_shape],
        scratch_shapes=[pltpu.VMEM((SUBLANES, D_RNN), F32), pltpu.VMEM((1, D_RNN), F32)],
        compiler_params=_cparams("parallel", "arbitrary"),
        name="c_mix",
    )(x, g2, g3, w_in, conv_w, conv_b, wgi, b_rg, b_ig, lam, w_out)


def _c_step_kernel(x_ref, g2_ref, g3_ref, win_ref, s0_ref, s1_ref, s2_ref, h0_ref, cw_ref, cb_ref, wgi_ref,
                   brg_ref, big_ref, lam_ref, wout_ref, o_ref, xr_ref, h_ref):
    x = x_ref[...]
    xn = _rms(x, g2_ref[...]).astype(BF16)
    gate = _dot(xn, win_ref[:, 0:D_RNN])
    xr = _dot(xn, win_ref[:, D_RNN:])
    xc = (s0_ref[...] * cw_ref[0:1, :] + s1_ref[...] * cw_ref[1:2, :] + s2_ref[...] * cw_ref[2:3, :]
          + xr * cw_ref[3:4, :] + cb_ref[...])
    a, u = _lru_gates(xc, wgi_ref, brg_ref, big_ref, lam_ref)
    h = a * h0_ref[...] + u
    z = _gelu_tanh(gate) * h
    o_ref[...] = x + _rms(_dot(z.astype(BF16), wout_ref[...]), g3_ref[...])
    xr_ref[...] = xr
    h_ref[...] = h


def _c_step(x, g2, g3, w_in, state, h0, conv_w, conv_b, wgi, b_rg, b_ig, lam, w_out):
    m = x.shape[0]
    rows = _row_spec(m, D_RNN)
    shape = jax.ShapeDtypeStruct((m, D_RNN), F32)
    return pl.pallas_call(
        _c_step_kernel,
        grid=(1,),
        in_specs=[rows, _full_spec((1, D_MODEL)), _full_spec((1, D_MODEL)), _full_spec((D_MODEL, 2 * D_RNN)),
                  rows, rows, rows, rows, _full_spec((CONV_C, D_RNN)), _full_spec((1, D_RNN)),
                  _full_spec((N_BLK_C, BS_C, 2 * BS_C)), _full_spec((1, D_RNN)), _full_spec((1, D_RNN)),
                  _full_spec((1, D_RNN)), _full_spec((D_RNN, D_MODEL))],
        out_specs=[rows, rows, rows],
        out_shape=[shape, shape, shape],
        compiler_params=_cparams("arbitrary"),
        name="c_step",
    )(x, g2, g3, w_in, state[0], state[1], state[2], h0, conv_w, conv_b, wgi, b_rg, b_ig, lam, w_out)


def kernel(x_prompt, x_sample, p_prompt, p_sample, cache_k, cache_v, cache_logf, state_conv_b, state_conv_c, state_h_c, page_table, norms, ffn1_wi, ffn1_wo, ffn2_wi, ffn2_wo, ple_w_pe, ple_w_pg, ab_w_in, ab_b_f, ab_conv_w, ab_w_out, c_w_in, c_conv_w, c_conv_b, c_w_rg, c_b_rg, c_w_ig, c_b_ig, c_lam, c_w_out):
    bp, s_len, _ = x_prompt.shape
    bs = x_sample.shape[0]
    depth = norms.shape[0]
    n_phys, n_ab = cache_k.shape[0], cache_k.shape[1]
    mp = bp * s_len
    tm_x = min(512, s_len)

    xp = x_prompt.reshape(mp, D_MODEL)
    xs = x_sample.reshape(bs, D_MODEL)
    pp = p_prompt.reshape(depth, mp, D_PLE)
    psm = p_sample.reshape(depth, bs, D_PLE)
    cache_kt = jnp.transpose(cache_k, (0, 1, 3, 4, 2)).reshape(n_phys, n_ab, D_A, PAGE_SIZE)
    cache_vt = jnp.transpose(cache_v, (0, 1, 3, 4, 2)).reshape(n_phys, n_ab, D_A, PAGE_SIZE)
    cache_ft = jnp.swapaxes(cache_logf, 2, 3)
    w_pe = ple_w_pe.astype(BF16)
    w_pg = ple_w_pg.astype(BF16)

    kv_p = kv_s = None
    fp_l, fs_l = [], []
    cbp_l, cbs_l, ccp_l, ccs_l, hp_l, hs_l = [], [], [], [], [], []
    for i in range(depth):
        g = norms[i].reshape(norms.shape[1], 1, D_MODEL)
        xs, w_ffn = _ffn_step(xs, g[0], g[1], ffn1_wi, ffn1_wo, i)
        xp = _ffn_half(xp, g[0], g[1], w_ffn, i, 2 * tm_x)
        l = i // 2
        if i % 2 == 0:
            w_in = ab_w_in[l]
            wa = jnp.concatenate([w_in[:, :D_A], w_in[:, 3 * D_A + H_A:]], axis=1).astype(BF16)
            wkvf = jnp.pad(w_in[:, D_A:3 * D_A + H_A].T, ((0, BF16_ROWS - H_A), (0, 0))).astype(BF16)
            b_f = ab_b_f[l].reshape(H_A, 1)
            w_out = ab_w_out[l].astype(BF16)
            q, kt, vt, ktb, vtb, lft, bg, cx = _ab_proj(xp, g[2], wa, wkvf, b_f, bp, tm_x, HD_A ** -0.5 * LOG2E,
                                                        l, n_ab, kv_p)
            kv_p = (kt, vt)
            nb = s_len // tm_x
            ck = _cumsum_lanes(lft).reshape(bp, N_PAIR, 2, nb, tm_x).transpose(0, 1, 3, 2, 4)
            ya = _fox_prompt(q, ktb, vtb, ck)
            mix = (g[3], ya, bg, cx, ab_conv_w[l], w_out, s_len)
            fp_l.append(lft)
            cbp_l.append(cx.reshape(bp, s_len, D_B)[:, s_len - (CONV_B - 1):])
            q, kt, vt, _, _, lft, bg, cx = _ab_proj(xs, g[2], wa, wkvf, b_f, 1, bs, HD_A ** -0.5, l, n_ab, kv_s)
            kv_s = (kt, vt)
            ya = _fox_decode(page_table, q.reshape(bs, 1, D_A), kt[0, l].T.reshape(bs, 1, D_A),
                             vt[0, l].T.reshape(bs, 1, D_A), lft[0].T.reshape(bs, H_A, 1),
                             cache_kt, cache_vt, cache_ft, l)
            st = state_conv_b[:, l]
            xs = _ab_step_out(xs, g[3], ya.reshape(bs, D_A), bg, cx, ab_conv_w[l], w_out, (st[:, 0], st[:, 1]))
            fs_l.append(lft[0])
            cbs_l.append(jnp.stack([st[:, 1], cx], axis=1))
        else:
            w_in = c_w_in[l].astype(BF16)
            wgi = jnp.concatenate([c_w_rg[l], c_w_ig[l]], axis=-1).astype(BF16)
            row = lambda a: a.reshape(1, D_RNN)
            gate_args = (c_conv_w[l], row(c_conv_b[l]), wgi, row(c_b_rg[l]), row(c_b_ig[l]), row(c_lam[l]),
                         c_w_out[l].astype(BF16))
            xp, xr_tail, h_tail = _c_mix(xp, g[2], g[3], w_in, *gate_args, bp, tm_x)
            mix = None
            ccp_l.append(xr_tail[:, SUBLANES - (CONV_C - 1):])
            hp_l.append(h_tail[:, SUBLANES - 1])
            st = state_conv_c[:, l]
            xs, xr, h = _c_step(xs, g[2], g[3], w_in, (st[:, 0], st[:, 1], st[:, 2]), state_h_c[:, l], *gate_args)
            ccs_l.append(jnp.stack([st[:, 1], st[:, 2], xr], axis=1))
            hs_l.append(h)
        xs, w_ffn = _ffn_step(xs, g[4], g[5], ffn2_wi, ffn2_wo, i, ple=(psm, g[6], w_pe, w_pg))
        xp = _ffn_half(xp, g[4], g[5], w_ffn, i, tm_x, ple=(pp, g[6], w_pe, w_pg), mix=mix)

    st = lambda lst: jnp.stack(lst, axis=1)
    out_p = lambda a: a.reshape(bp, n_ab, H_A, HD_A, s_len).transpose(0, 1, 4, 2, 3)
    out_s = lambda a: a.reshape(n_ab, 1, H_A, HD_A, bs).transpose(4, 0, 1, 2, 3)
    return (xp.reshape(bp, s_len, D_MODEL), xs.reshape(bs, 1, D_MODEL),
            out_p(kv_p[0]), out_p(kv_p[1]), st(fp_l).transpose(0, 1, 3, 2),
            out_s(kv_s[0]), out_s(kv_s[1]), jnp.stack(fs_l, axis=0).reshape(n_ab, 1, H_A, bs).transpose(3, 0, 1, 2),
            st(cbp_l), st(cbs_l), st(ccp_l), st(ccs_l), st(hp_l), st(hs_l))
```

```python
import functools

import jax
import jax.numpy as jnp
from jax import lax
from jax.experimental import pallas as pl
from jax.experimental.pallas import tpu as pltpu

F32 = jnp.float32
BF16 = jnp.bfloat16

D_MODEL = 1024
D_PLE = 256
H_A = 8
HD_A = 64
D_A = H_A * HD_A
D_B = D_MODEL - D_A
CONV_B = 3
D_RNN = D_MODEL
N_BLK_C = 8
BS_C = D_RNN // N_BLK_C
CONV_C = 4
LRU_C = 8.0
D_FF = 2816
EPS = 1e-6
NEG = -1e30
PAGE_SIZE = 128

LANES = 128
SUBLANES = 8
BF16_ROWS = 16
FF_CHUNKS = ((0, 1024), (1024, 1024), (2048, 768))
FF_STEP = 256
FFN_ROWS = 256
C_ROWS = 256
LOG2E = 1.4426950408889634
DECODE_SEQS = 2
N_PAIR = D_A // LANES
KVF_ROWS = 2 * D_A + BF16_ROWS
VMEM_LIMIT = 56 * 1024 * 1024

_NT = (((1,), (1,)), ((), ()))


def _cparams(*sem):
    return pltpu.CompilerParams(dimension_semantics=sem, vmem_limit_bytes=VMEM_LIMIT)


def _rms(x, g):
    return x * lax.rsqrt(jnp.mean(x * x, axis=-1, keepdims=True) + EPS) * g


def _dot(a, b):
    return jnp.dot(a, b, preferred_element_type=F32)


def _dot_nt(a, b):
    return lax.dot_general(a, b, _NT, preferred_element_type=F32)


def _log_sigmoid(x):
    return jnp.minimum(x, 0.0) - jnp.log1p(jnp.exp(-jnp.abs(x)))


def _softplus(x):
    return jnp.maximum(x, 0.0) + jnp.log1p(jnp.exp(-jnp.abs(x)))


def _sigmoid(x):
    return 1.0 / (1.0 + jnp.exp(-x))


def _gelu_tanh(x):
    c = 0.7978845608028654
    hx = 0.5 * x
    return hx + hx * jnp.tanh(x * (c + (0.044715 * c) * (x * x)))


def _row_spec(tm, d):
    return pl.BlockSpec((tm, d), lambda i: (i, 0))


def _full_spec(shape):
    nd = len(shape)
    return pl.BlockSpec(shape, lambda i: (0,) * nd)


def _shifted_rows(cur, prev, n_shift):
    sub = lax.broadcasted_iota(jnp.int32, (SUBLANES, 1), 0)
    out = []
    for sft in range(1, n_shift + 1):
        r = pltpu.roll(cur, sft, 0)
        top = r[0:SUBLANES, :]
        for i in range(sft):
            top = jnp.where(sub == i, prev[SUBLANES - sft + i:SUBLANES - sft + i + 1, :], top)
        out.append(jnp.concatenate([top, r[SUBLANES:, :]], axis=0))
    return out


def _short_conv_out(x, ya, bg, cx, m1, m2, cw_ref, wo_ref, g):
    conv = m2 * cw_ref[0:1, :] + m1 * cw_ref[1:2, :] + cx * cw_ref[2:3, :]
    yb = bg * conv
    y = _dot(ya, wo_ref[0:D_A, :]) + _dot(yb.astype(BF16), wo_ref[D_A:, :])
    return x + _rms(y, g)


def _ffn_kernel(*refs, with_mix, with_ple, tiles_per_seq):
    refs = list(refs)
    x = refs.pop(0)[...]
    if with_mix:
        gmix_ref, ya_ref, bg_ref, cx_ref, prev_ref, cw_ref, wmix_ref = refs[:7]
        del refs[:7]
        cx = cx_ref[...]
        first = (pl.program_id(0) % tiles_per_seq) == 0
        m1, m2 = _shifted_rows(cx, prev_ref[...] * jnp.where(first, 0.0, 1.0), CONV_B - 1)
        x = _short_conv_out(x, ya_ref[...], bg_ref[...], cx, m1, m2, cw_ref, wmix_ref, gmix_ref[...])
    gpre_ref, gpost_ref, wig_ref, wiu_ref, wo_ref = refs[:5]
    del refs[:5]
    n_grp = max(1, x.shape[0] // FFN_ROWS)
    xg = [x[r * FFN_ROWS:(r + 1) * FFN_ROWS] for r in range(n_grp)] if n_grp > 1 else [x]
    xn = [_rms(xr, gpre_ref[...]).astype(BF16) for xr in xg]
    acc = [None] * n_grp
    for c0, cw in FF_CHUNKS:
        for r in range(n_grp):
            g = _dot(xn[r], wig_ref[:, c0:c0 + cw])
            u = _dot(xn[r], wiu_ref[:, c0:c0 + cw])
            h = (g * _sigmoid(g)) * u
            part = _dot(h.astype(BF16), wo_ref[c0:c0 + cw, :])
            acc[r] = part if acc[r] is None else acc[r] + part
    xg = [xr + 0.5 * _rms(a, gpost_ref[...]) for xr, a in zip(xg, acc)]
    x = jnp.concatenate(xg, axis=0) if n_grp > 1 else xg[0]
    if with_ple:
        x = _ple(x, *refs[:4])
        del refs[:4]
    o_ref, = refs
    o_ref[...] = x


def _ple(x, p_ref, g_ref, wpe_ref, wpg_ref):
    gate = _sigmoid(_dot(_rms(x, g_ref[...]).astype(BF16), wpg_ref[...]))
    return x + _dot(p_ref[...].astype(BF16), wpe_ref[...]) * gate


def _ple_specs(layer, tm, index):
    resident = lambda shape: pl.BlockSpec((None,) + shape, lambda *_: (layer, 0, 0), pipeline_mode=pl.Buffered(1))
    return [pl.BlockSpec((None, tm, D_PLE), lambda *i: (layer,) + index(*i)),
            pl.BlockSpec((1, D_MODEL), lambda *_: (0, 0)), resident((D_PLE, D_MODEL)), resident((D_MODEL, D_MODEL))]


def _ffn_half(x, g_pre, g_post, w, layer, tm, ple=None, mix=None):
    m = x.shape[0]
    resident = lambda shape: pl.BlockSpec(shape, lambda i: (0, 0), pipeline_mode=pl.Buffered(1))
    in_specs = [_row_spec(tm, D_MODEL)]
    args = [x]
    tiles_per_seq = 1
    if mix is not None:
        g_mix, ya, bg, cx, conv_w, w_out, seq_len = mix
        tiles_per_seq = seq_len // tm
        per = tm // SUBLANES
        in_specs += [_full_spec((1, D_MODEL)), _row_spec(tm, D_A), _row_spec(tm, D_B), _row_spec(tm, D_B),
                     pl.BlockSpec((SUBLANES, D_B), lambda i: (jnp.maximum(i * per - 1, 0), 0)),
                     _full_spec((CONV_B, D_B)), _full_spec((D_MODEL, D_MODEL))]
        args += [g_mix, ya, bg, cx, cx, conv_w, w_out]
    in_specs += [_full_spec((1, D_MODEL)), _full_spec((1, D_MODEL)),
                 resident((D_MODEL, D_FF)), resident((D_MODEL, D_FF)), resident((D_FF, D_MODEL))]
    args += [g_pre, g_post, *w]
    if ple is not None:
        in_specs += _ple_specs(layer, tm, lambda i: (i, 0))
        args += list(ple)
    return pl.pallas_call(
        functools.partial(_ffn_kernel, with_mix=mix is not None, with_ple=ple is not None,
                          tiles_per_seq=tiles_per_seq),
        grid=(m // tm,),
        in_specs=in_specs,
        out_specs=_row_spec(tm, D_MODEL),
        out_shape=jax.ShapeDtypeStruct((m, D_MODEL), F32),
        compiler_params=_cparams("parallel"),
        name="ffn_half",
    )(*args)


def _ffn_step_kernel(*refs, with_ple):
    refs = list(refs)
    x_ref, gpre_ref, gpost_ref, wig_ref, wiu_ref, wo_ref = refs[:6]
    ple_refs = refs[6:10] if with_ple else []
    o_ref, wigb_ref, wiub_ref, wob_ref, xn_ref, acc_ref = refs[-6:]
    j = pl.program_id(0)

    @pl.when(j == 0)
    def _():
        xn_ref[...] = _rms(x_ref[...], gpre_ref[...]).astype(BF16)
        acc_ref[...] = jnp.zeros_like(acc_ref)

    wig = wig_ref[...].astype(BF16)
    wiu = wiu_ref[...].astype(BF16)
    wo = wo_ref[...].astype(BF16)
    wigb_ref[...] = wig
    wiub_ref[...] = wiu
    wob_ref[...] = wo
    xn = xn_ref[...]
    g = _dot(xn, wig)
    u = _dot(xn, wiu)
    acc_ref[...] += _dot(((g * _sigmoid(g)) * u).astype(BF16), wo)

    @pl.when(j == pl.num_programs(0) - 1)
    def _():
        x = x_ref[...] + 0.5 * _rms(acc_ref[...], gpost_ref[...])
        if with_ple:
            x = _ple(x, *ple_refs)
        o_ref[...] = x


def _ffn_step(x, g_pre, g_post, wi, wo, layer, ple=None):
    m = x.shape[0]
    nf = D_FF // FF_STEP
    const = lambda shape: pl.BlockSpec(shape, lambda j: (0,) * len(shape))
    in_specs = [const((m, D_MODEL)), const((1, D_MODEL)), const((1, D_MODEL)),
                pl.BlockSpec((None, D_MODEL, FF_STEP), lambda j: (layer, 0, j)),
                pl.BlockSpec((None, D_MODEL, FF_STEP), lambda j: (layer, 0, j + nf)),
                pl.BlockSpec((None, FF_STEP, D_MODEL), lambda j: (layer, j, 0))]
    args = [x, g_pre, g_post, wi, wi, wo]
    if ple is not None:
        in_specs += _ple_specs(layer, m, lambda j: (0, 0))
        args += list(ple)
    up_spec = pl.BlockSpec((D_MODEL, FF_STEP), lambda j: (0, j))
    up_shape = jax.ShapeDtypeStruct((D_MODEL, D_FF), BF16)
    out = pl.pallas_call(
        functools.partial(_ffn_step_kernel, with_ple=ple is not None),
        grid=(nf,),
        in_specs=in_specs,
        out_specs=[const((m, D_MODEL)), up_spec, up_spec, pl.BlockSpec((FF_STEP, D_MODEL), lambda j: (j, 0))],
        out_shape=[jax.ShapeDtypeStruct((m, D_MODEL), F32), up_shape, up_shape,
                   jax.ShapeDtypeStruct((D_FF, D_MODEL), BF16)],
        scratch_shapes=[pltpu.VMEM((m, D_MODEL), BF16), pltpu.VMEM((m, D_MODEL), F32)],
        compiler_params=_cparams("arbitrary"),
        name="ffn_step",
    )(*args)
    return out[0], tuple(out[1:])


def _ab_proj_kernel(*refs, q_scale):
    x_ref, g_ref, wa_ref, wkvf_ref, bf_ref = refs[:5]
    q_ref, kt_ref, vt_ref, ktb_ref, vtb_ref, lft_ref, bg_ref, cx_ref = refs[-8:]
    tm = x_ref.shape[0]
    xn = _rms(x_ref[...], g_ref[...]).astype(BF16)
    col = lambda c: _dot(xn, wa_ref[:, c * D_A:(c + 1) * D_A])
    q_ref[...] = (col(0) * q_scale).astype(BF16)
    bg_ref[...] = col(1)
    cx_ref[...] = col(2) * col(3)
    kvf = _dot_nt(wkvf_ref[...], xn)
    kt = kvf[0:D_A]
    vt = kvf[D_A:2 * D_A]
    kt_ref[...] = kt
    vt_ref[...] = vt
    ktb_ref[...] = kt.astype(BF16).reshape(N_PAIR, LANES, tm)
    vtb_ref[:, 0:HD_A, :] = vt.astype(BF16).reshape(H_A, HD_A, tm)
    ones_row = lax.broadcasted_iota(jnp.int32, (H_A, LANES - HD_A, tm), 1) == 0
    vtb_ref[:, HD_A:, :] = jnp.where(ones_row, 1.0, 0.0).astype(BF16)
    lft_ref[...] = _log_sigmoid(kvf[2 * D_A:2 * D_A + H_A] + bf_ref[...])


def _ab_proj(x, g, wa, wkvf, b_f, n_seq, tm, q_scale, layer, n_layers, kv_prev=None):
    m = x.shape[0]
    s_len = m // n_seq
    nt = s_len // tm
    rows = lambda d: pl.BlockSpec((tm, d), lambda b, t: (b * nt + t, 0))
    full = lambda shape: pl.BlockSpec(shape, lambda b, t: (0,) * len(shape))
    kv_spec = pl.BlockSpec((None, None, D_A, tm), lambda b, t: (b, layer, 0, t))
    kv_shape = jax.ShapeDtypeStruct((n_seq, n_layers, D_A, s_len), F32)
    tile_spec = lambda n: pl.BlockSpec((None, n, None, LANES, tm), lambda b, t: (b, 0, t, 0, 0))
    wide = lambda dt: jax.ShapeDtypeStruct((m, D_A), dt)
    tile_shape = lambda n: jax.ShapeDtypeStruct((n_seq, n, nt, LANES, tm), BF16)
    in_specs = [rows(D_MODEL), full((1, D_MODEL)), full((D_MODEL, 4 * D_A)), full((KVF_ROWS, D_MODEL)),
                full((H_A, 1))]
    args = [x, g, wa, wkvf, b_f]
    aliases = {}
    if kv_prev is not None:
        in_specs += [pl.BlockSpec(memory_space=pl.ANY)] * 2
        args += list(kv_prev)
        aliases = {5: 1, 6: 2}
    return pl.pallas_call(
        functools.partial(_ab_proj_kernel, q_scale=q_scale),
        grid=(n_seq, nt),
        in_specs=in_specs,
        out_specs=[rows(D_A), kv_spec, kv_spec, tile_spec(N_PAIR), tile_spec(H_A),
                   pl.BlockSpec((None, H_A, tm), lambda b, t: (b, 0, t)), rows(D_B), rows(D_B)],
        out_shape=[wide(BF16), kv_shape, kv_shape, tile_shape(N_PAIR), tile_shape(H_A),
                   jax.ShapeDtypeStruct((n_seq, H_A, s_len), F32), wide(F32), wide(F32)],
        input_output_aliases=aliases,
        compiler_params=_cparams("parallel", "parallel"),
        name="ab_proj",
    )(*args)


def _cumsum_kernel(x_ref, o_ref):
    x = x_ref[...]
    n = x.shape[1]
    pos = lax.broadcasted_iota(jnp.int32, (1, n), 1)
    d = 1
    while d < n:
        x = x + jnp.where(pos >= d, pltpu.roll(x, d, 1), 0.0)
        d *= 2
    o_ref[...] = x * LOG2E


def _cumsum_lanes(x):
    spec = pl.BlockSpec((None,) + x.shape[1:], lambda b: (b, 0, 0))
    return pl.pallas_call(
        _cumsum_kernel,
        grid=(x.shape[0],),
        in_specs=[spec],
        out_specs=spec,
        out_shape=jax.ShapeDtypeStruct(x.shape, F32),
        compiler_params=_cparams("parallel"),
        name="logf_cumsum",
    )(x)


def _fox_prompt_kernel(q_ref, kt_ref, vt_ref, ck_ref, o_ref):
    qi = pl.program_id(2)
    t = q_ref.shape[0]
    lane = lax.broadcasted_iota(jnp.int32, (1, LANES), 1)
    q = q_ref[...]
    zero = jnp.zeros_like(q)
    qs = (jnp.where(lane < HD_A, q, zero), jnp.where(lane >= HD_A, q, zero))

    def tile(j, state, diagonal):
        kt = kt_ref[j]
        ck = ck_ref[j]
        out = []
        scores = [_dot(qs[e], kt) - ck[e:e + 1, :] for e in range(2)]
        for e in range(2):
            m_prev, acc_prev = state[e]
            s = scores[e]
            if diagonal:
                row = lax.broadcasted_iota(jnp.int32, (t, 1), 0)
                colk = lax.broadcasted_iota(jnp.int32, (1, t), 1)
                s = jnp.where(colk <= row, s, NEG)
            m_new = jnp.maximum(m_prev, jnp.max(s, axis=-1, keepdims=True))
            p = jnp.exp2(s - m_new).astype(BF16)
            acc = jnp.exp2(m_prev - m_new) * acc_prev + _dot_nt(p, vt_ref[e, j])
            out.append((m_new, acc))
        return tuple(out)

    init = tuple((jnp.full((t, 1), NEG, F32), jnp.zeros((t, LANES), F32)) for _ in range(2))
    state = lax.fori_loop(0, qi // 2, lambda i, st: tile(2 * i + 1, tile(2 * i, st, False), False), init)
    (_, a0), (_, a1) = lax.cond(qi % 2 == 1, lambda st: tile(qi, tile(qi - 1, st, False), True),
                                lambda st: tile(qi, st, True), state)
    o0 = a0[:, 0:HD_A] / a0[:, HD_A:HD_A + 1]
    o1 = a1[:, 0:HD_A] / a1[:, HD_A:HD_A + 1]
    o_ref[...] = jnp.concatenate([o0, o1], axis=-1).astype(BF16)


def _fox_prompt(q, ktb, vtb, ck):
    n_seq, _, nb, _, t = ktb.shape
    return pl.pallas_call(
        _fox_prompt_kernel,
        grid=(n_seq, N_PAIR, nb),
        in_specs=[pl.BlockSpec((t, LANES), lambda b, h, i: (b * nb + i, h)),
                  pl.BlockSpec((None, None, nb, LANES, t), lambda b, h, i: (b, h, 0, 0, 0)),
                  pl.BlockSpec((None, 2, nb, LANES, t), lambda b, h, i: (b, h, 0, 0, 0)),
                  pl.BlockSpec((None, None, nb, 2, t), lambda b, h, i: (b, h, 0, 0, 0))],
        out_specs=pl.BlockSpec((t, LANES), lambda b, h, i: (b * nb + i, h)),
        out_shape=jax.ShapeDtypeStruct(q.shape, BF16),
        compiler_params=_cparams("parallel", "parallel", "parallel"),
        name="fox_prompt",
    )(q, ktb, vtb, ck)


def _fox_decode_kernel(pt_ref, q_ref, kn_ref, vn_ref, fn_ref, ft_ref, *refs, n_pages, n_seq):
    k_refs = refs[:n_seq * n_pages]
    v_refs = refs[n_seq * n_pages:2 * n_seq * n_pages]
    o_ref = refs[2 * n_seq * n_pages]
    ps = PAGE_SIZE
    b0 = pl.program_id(0) * n_seq

    lane = lax.broadcasted_iota(jnp.int32, (H_A, D_A), 1)
    hrow = lax.broadcasted_iota(jnp.int32, (H_A, D_A), 0)
    own = (lane // HD_A) == hrow
    pos = lax.broadcasted_iota(jnp.int32, (H_A, ps), 1)

    for sq in range(n_seq):
        qbd_f = jnp.where(own, q_ref[sq].astype(F32), 0.0)
        qbd = qbd_f.astype(BF16)

        carry = fn_ref[sq]
        bias = [None] * n_pages
        for j in reversed(range(n_pages)):
            lf = ft_ref[pt_ref[b0 + sq, j]]
            x = lf
            d = 1
            while d < ps:
                x = x + jnp.where(pos + d < ps, pltpu.roll(x, ps - d, 1), 0.0)
                d *= 2
            bias[j] = (x - lf) + carry
            carry = carry + x[:, 0:1]

        kn = kn_ref[sq].astype(BF16).astype(F32)
        s_new = jnp.sum(qbd_f * kn, axis=-1, keepdims=True)
        scores = []
        m = s_new
        for j in range(n_pages):
            s = _dot(qbd, k_refs[sq * n_pages + j][...].astype(BF16)) + bias[j]
            scores.append(s)
            m = jnp.maximum(m, jnp.max(s, axis=-1, keepdims=True))
        p_new = jnp.exp(s_new - m)
        l = p_new
        vn = vn_ref[sq].astype(BF16).astype(F32)
        acc = p_new.astype(BF16).astype(F32) * vn
        for j in range(n_pages):
            p = jnp.exp(scores[j] - m)
            l = l + jnp.sum(p, axis=-1, keepdims=True)
            acc = acc + _dot_nt(p.astype(BF16), v_refs[sq * n_pages + j][...].astype(BF16))
        o = jnp.where(own, acc / l, 0.0)
        o_ref[sq] = jnp.sum(o, axis=0, keepdims=True).astype(BF16)


def _fox_decode(page_table, q, k_new, v_new, f_new, cache_kt, cache_vt, cache_ft, layer):
    bsz, n_pages = page_table.shape
    n_seq = DECODE_SEQS
    n_phys = cache_ft.shape[0]
    row = lambda dt_shape: pl.BlockSpec((n_seq,) + dt_shape, lambda b, pt: (b, 0, 0))
    kv_specs = [pl.BlockSpec((None, None, D_A, PAGE_SIZE), functools.partial(
        lambda b, pt, sq, j: (pt[b * n_seq + sq, j], layer, 0, 0), sq=sq, j=j))
        for sq in range(n_seq) for j in range(n_pages)]
    f_spec = pl.BlockSpec((n_phys, None, H_A, PAGE_SIZE), lambda b, pt: (0, layer, 0, 0),
                          pipeline_mode=pl.Buffered(1))
    grid_spec = pltpu.PrefetchScalarGridSpec(
        num_scalar_prefetch=1,
        grid=(bsz // n_seq,),
        in_specs=[row((1, D_A)), row((1, D_A)), row((1, D_A)), row((H_A, 1)), f_spec] + kv_specs + kv_specs,
        out_specs=row((1, D_A)),
    )
    n_kv = n_seq * n_pages
    return pl.pallas_call(
        functools.partial(_fox_decode_kernel, n_pages=n_pages, n_seq=n_seq),
        grid_spec=grid_spec,
        out_shape=jax.ShapeDtypeStruct((bsz, 1, D_A), BF16),
        compiler_params=_cparams("parallel"),
        name="fox_decode",
    )(page_table, q, k_new, v_new, f_new, cache_ft, *([cache_kt] * n_kv), *([cache_vt] * n_kv))


def _ab_step_out_kernel(x_ref, g_ref, ya_ref, bg_ref, cx_ref, m1_ref, m2_ref, cw_ref, wo_ref, o_ref):
    o_ref[...] = _short_conv_out(x_ref[...], ya_ref[...], bg_ref[...], cx_ref[...], m1_ref[...], m2_ref[...],
                                 cw_ref, wo_ref, g_ref[...])


def _ab_step_out(x, g, ya, bg, cx, conv_w, w_out, state):
    m = x.shape[0]
    return pl.pallas_call(
        _ab_step_out_kernel,
        grid=(1,),
        in_specs=[_row_spec(m, D_MODEL), _full_spec((1, D_MODEL)), _row_spec(m, D_A), _row_spec(m, D_B),
                  _row_spec(m, D_B), _row_spec(m, D_B), _row_spec(m, D_B), _full_spec((CONV_B, D_B)),
                  _full_spec((D_MODEL, D_MODEL))],
        out_specs=_row_spec(m, D_MODEL),
        out_shape=jax.ShapeDtypeStruct((m, D_MODEL), F32),
        compiler_params=_cparams("arbitrary"),
        name="ab_step_out",
    )(x, g, ya, bg, cx, state[1], state[0], conv_w, w_out)


def _lru_gates(xc, wgi_ref, brg_ref, big_ref, lam_ref):
    r_parts, i_parts = [], []
    for n in range(N_BLK_C):
        ri = _dot(xc[:, n * BS_C:(n + 1) * BS_C].astype(BF16), wgi_ref[n])
        r_parts.append(ri[:, :BS_C])
        i_parts.append(ri[:, BS_C:])
    r = _sigmoid(jnp.concatenate(r_parts, axis=-1) + brg_ref[...])
    ig = _sigmoid(jnp.concatenate(i_parts, axis=-1) + big_ref[...])
    log_a = r * (-LRU_C * _softplus(-lam_ref[...]))
    a = jnp.exp(log_a)
    th = jnp.tanh(log_a)
    u = jnp.sqrt(-2.0 * th / (1.0 - th)) * (ig * xc)
    return a, u


def _c_mix_kernel(x_ref, g2_ref, g3_ref, win_ref, cw_ref, cb_ref, wgi_ref, brg_ref, big_ref, lam_ref, wout_ref,
                  o_ref, xr_tail_ref, h_tail_ref, prev_ref, h_ref):
    tm = x_ref.shape[0]
    rows = min(C_ROWS, tm)
    ngrp = rows // SUBLANES

    @pl.when(pl.program_id(1) == 0)
    def _():
        prev_ref[...] = jnp.zeros_like(prev_ref)
        h_ref[...] = jnp.zeros_like(h_ref)

    prev = prev_ref[...]
    h = h_ref[...]
    sub = lax.broadcasted_iota(jnp.int32, (1, SUBLANES, 1), 1)
    outs = []
    proj = []
    for r0 in range(0, tm, rows):
        x = x_ref[r0:r0 + rows, :]
        xn = _rms(x, g2_ref[...]).astype(BF16)
        proj.append((x, _dot(xn, win_ref[:, 0:D_RNN]), _dot(xn, win_ref[:, D_RNN:])))
    for x, gate, xr in proj:
        m1, m2, m3 = _shifted_rows(xr, prev, CONV_C - 1)
        prev = xr[rows - SUBLANES:, :]
        xc = m3 * cw_ref[0:1, :] + m2 * cw_ref[1:2, :] + m1 * cw_ref[2:3, :] + xr * cw_ref[3:4, :] + cb_ref[...]
        a, u = _lru_gates(xc, wgi_ref, brg_ref, big_ref, lam_ref)

        a = a.reshape(ngrp, SUBLANES, D_RNN)
        u = u.reshape(ngrp, SUBLANES, D_RNN)
        d = 1
        while d < SUBLANES:
            ok = sub >= d
            a_sh = jnp.where(ok, pltpu.roll(a, d, 1), 1.0)
            u_sh = jnp.where(ok, pltpu.roll(u, d, 1), 0.0)
            u = a * u_sh + u
            a = a * a_sh
            d *= 2
        groups = []
        for i in range(ngrp):
            hs_i = a[i] * h + u[i]
            h = hs_i[SUBLANES - 1:SUBLANES, :]
            groups.append(hs_i)
        hs = jnp.concatenate(groups, axis=0)
        z = _gelu_tanh(gate) * hs
        outs.append(x + _rms(_dot(z.astype(BF16), wout_ref[...]), g3_ref[...]))
    o_ref[...] = jnp.concatenate(outs, axis=0)
    prev_ref[...] = prev
    h_ref[...] = h
    xr_tail_ref[...] = prev
    h_tail_ref[...] = hs[rows - SUBLANES:, :]


def _c_mix(x, g2, g3, w_in, conv_w, conv_b, wgi, b_rg, b_ig, lam, w_out, n_seq, tm):
    m = x.shape[0]
    nt = m // n_seq // tm
    rows = pl.BlockSpec((tm, D_MODEL), lambda b, t: (b * nt + t, 0))
    full = lambda shape: pl.BlockSpec(shape, lambda b, t: (0,) * len(shape))
    tail = pl.BlockSpec((None, SUBLANES, D_RNN), lambda b, t: (b, 0, 0))
    tail_shape = jax.ShapeDtypeStruct((n_seq, SUBLANES, D_RNN), F32)
    return pl.pallas_call(
        _c_mix_kernel,
        grid=(n_seq, nt),
        in_specs=[rows, full((1, D_MODEL)), full((1, D_MODEL)), full((D_MODEL, 2 * D_RNN)), full((CONV_C, D_RNN)),
                  full((1, D_RNN)), full((N_BLK_C, BS_C, 2 * BS_C)), full((1, D_RNN)), full((1, D_RNN)),
                  full((1, D_RNN)), full((D_RNN, D_MODEL))],
        out_specs=[rows, tail, tail],
        out_shape=[jax.ShapeDtypeStruct((m, D_MODEL), F32), tail_shape, tail_shape],
        scratch_shapes=[pltpu.VMEM((SUBLANES, D_RNN), F32), pltpu.VMEM((1, D_RNN), F32)],
        compiler_params=_cparams("parallel", "arbitrary"),
        name="c_mix",
    )(x, g2, g3, w_in, conv_w, conv_b, wgi, b_rg, b_ig, lam, w_out)


def _c_step_kernel(x_ref, g2_ref, g3_ref, win_ref, s0_ref, s1_ref, s2_ref, h0_ref, cw_ref, cb_ref, wgi_ref,
                   brg_ref, big_ref, lam_ref, wout_ref, o_ref, xr_ref, h_ref):
    x = x_ref[...]
    xn = _rms(x, g2_ref[...]).astype(BF16)
    gate = _dot(xn, win_ref[:, 0:D_RNN])
    xr = _dot(xn, win_ref[:, D_RNN:])
    xc = (s0_ref[...] * cw_ref[0:1, :] + s1_ref[...] * cw_ref[1:2, :] + s2_ref[...] * cw_ref[2:3, :]
          + xr * cw_ref[3:4, :] + cb_ref[...])
    a, u = _lru_gates(xc, wgi_ref, brg_ref, big_ref, lam_ref)
    h = a * h0_ref[...] + u
    z = _gelu_tanh(gate) * h
    o_ref[...] = x + _rms(_dot(z.astype(BF16), wout_ref[...]), g3_ref[...])
    xr_ref[...] = xr
    h_ref[...] = h


def _c_step(x, g2, g3, w_in, state, h0, conv_w, conv_b, wgi, b_rg, b_ig, lam, w_out):
    m = x.shape[0]
    rows = _row_spec(m, D_RNN)
    shape = jax.ShapeDtypeStruct((m, D_RNN), F32)
    return pl.pallas_call(
        _c_step_kernel,
        grid=(1,),
        in_specs=[rows, _full_spec((1, D_MODEL)), _full_spec((1, D_MODEL)), _full_spec((D_MODEL, 2 * D_RNN)),
                  rows, rows, rows, rows, _full_spec((CONV_C, D_RNN)), _full_spec((1, D_RNN)),
                  _full_spec((N_BLK_C, BS_C, 2 * BS_C)), _full_spec((1, D_RNN)), _full_spec((1, D_RNN)),
                  _full_spec((1, D_RNN)), _full_spec((D_RNN, D_MODEL))],
        out_specs=[rows, rows, rows],
        out_shape=[shape, shape, shape],
        compiler_params=_cparams("arbitrary"),
        name="c_step",
    )(x, g2, g3, w_in, state[0], state[1], state[2], h0, conv_w, conv_b, wgi, b_rg, b_ig, lam, w_out)


def kernel(x_prompt, x_sample, p_prompt, p_sample, cache_k, cache_v, cache_logf, state_conv_b, state_conv_c, state_h_c, page_table, norms, ffn1_wi, ffn1_wo, ffn2_wi, ffn2_wo, ple_w_pe, ple_w_pg, ab_w_in, ab_b_f, ab_conv_w, ab_w_out, c_w_in, c_conv_w, c_conv_b, c_w_rg, c_b_rg, c_w_ig, c_b_ig, c_lam, c_w_out):
    bp, s_len, _ = x_prompt.shape
    bs = x_sample.shape[0]
    depth = norms.shape[0]
    n_phys, n_ab = cache_k.shape[0], cache_k.shape[1]
    mp = bp * s_len
    tm_x = min(512, s_len)

    xp = x_prompt.reshape(mp, D_MODEL)
    xs = x_sample.reshape(bs, D_MODEL)
    pp = p_prompt.reshape(depth, mp, D_PLE)
    psm = p_sample.reshape(depth, bs, D_PLE)
    cache_kt = jnp.transpose(cache_k, (0, 1, 3, 4, 2)).reshape(n_phys, n_ab, D_A, PAGE_SIZE)
    cache_vt = jnp.transpose(cache_v, (0, 1, 3, 4, 2)).reshape(n_phys, n_ab, D_A, PAGE_SIZE)
    cache_ft = jnp.swapaxes(cache_logf, 2, 3)
    w_pe = ple_w_pe.astype(BF16)
    w_pg = ple_w_pg.astype(BF16)

    kv_p = kv_s = None
    fp_l, fs_l = [], []
    cbp_l, cbs_l, ccp_l, ccs_l, hp_l, hs_l = [], [], [], [], [], []
    for i in range(depth):
        g = norms[i].reshape(norms.shape[1], 1, D_MODEL)
        xs, w_ffn = _ffn_step(xs, g[0], g[1], ffn1_wi, ffn1_wo, i)
        xp = _ffn_half(xp, g[0], g[1], w_ffn, i, 2 * tm_x)
        l = i // 2
        if i % 2 == 0:
            w_in = ab_w_in[l]
            wa = jnp.concatenate([w_in[:, :D_A], w_in[:, 3 * D_A + H_A:]], axis=1).astype(BF16)
            wkvf = jnp.pad(w_in[:, D_A:3 * D_A + H_A].T, ((0, BF16_ROWS - H_A), (0, 0))).astype(BF16)
            b_f = ab_b_f[l].reshape(H_A, 1)
            w_out = ab_w_out[l].astype(BF16)
            q, kt, vt, ktb, vtb, lft, bg, cx = _ab_proj(xp, g[2], wa, wkvf, b_f, bp, tm_x, HD_A ** -0.5 * LOG2E,
                                                        l, n_ab, kv_p)
            kv_p = (kt, vt)
            nb = s_len // tm_x
            ck = _cumsum_lanes(lft).reshape(bp, N_PAIR, 2, nb, tm_x).transpose(0, 1, 3, 2, 4)
            ya = _fox_prompt(q, ktb, vtb, ck)
            mix = (g[3], ya, bg, cx, ab_conv_w[l], w_out, s_len)
            fp_l.append(lft)
            cbp_l.append(cx.reshape(bp, s_len, D_B)[:, s_len - (CONV_B - 1):])
            q, kt, vt, _, _, lft, bg, cx = _ab_proj(xs, g[2], wa, wkvf, b_f, 1, bs, HD_A ** -0.5, l, n_ab, kv_s)
            kv_s = (kt, vt)
            ya = _fox_decode(page_table, q.reshape(bs, 1, D_A), kt[0, l].T.reshape(bs, 1, D_A),
                             vt[0, l].T.reshape(bs, 1, D_A), lft[0].T.reshape(bs, H_A, 1),
                             cache_kt, cache_vt, cache_ft, l)
            st = state_conv_b[:, l]
            xs = _ab_step_out(xs, g[3], ya.reshape(bs, D_A), bg, cx, ab_conv_w[l], w_out, (st[:, 0], st[:, 1]))
            fs_l.append(lft[0])
            cbs_l.append(jnp.stack([st[:, 1], cx], axis=1))
        else:
            w_in = c_w_in[l].astype(BF16)
            wgi = jnp.concatenate([c_w_rg[l], c_w_ig[l]], axis=-1).astype(BF16)
            row = lambda a: a.reshape(1, D_RNN)
            gate_args = (c_conv_w[l], row(c_conv_b[l]), wgi, row(c_b_rg[l]), row(c_b_ig[l]), row(c_lam[l]),
                         c_w_out[l].astype(BF16))
            xp, xr_tail, h_tail = _c_mix(xp, g[2], g[3], w_in, *gate_args, bp, 2 * tm_x)
            mix = None
            ccp_l.append(xr_tail[:, SUBLANES - (CONV_C - 1):])
            hp_l.append(h_tail[:, SUBLANES - 1])
            st = state_conv_c[:, l]
            xs, xr, h = _c_step(xs, g[2], g[3], w_in, (st[:, 0], st[:, 1], st[:, 2]), state_h_c[:, l], *gate_args)
            ccs_l.append(jnp.stack([st[:, 1], st[:, 2], xr], axis=1))
            hs_l.append(h)
        xs, w_ffn = _ffn_step(xs, g[4], g[5], ffn2_wi, ffn2_wo, i, ple=(psm, g[6], w_pe, w_pg))
        xp = _ffn_half(xp, g[4], g[5], w_ffn, i, tm_x, ple=(pp, g[6], w_pe, w_pg), mix=mix)

    st = lambda lst: jnp.stack(lst, axis=1)
    out_p = lambda a: a.reshape(bp, n_ab, H_A, HD_A, s_len).transpose(0, 1, 4, 2, 3)
    out_s = lambda a: a.reshape(n_ab, 1, H_A, HD_A, bs).transpose(4, 0, 1, 2, 3)
    return (xp.reshape(bp, s_len, D_MODEL), xs.reshape(bs, 1, D_MODEL),
            out_p(kv_p[0]), out_p(kv_p[1]), st(fp_l).transpose(0, 1, 3, 2),
            out_s(kv_s[0]), out_s(kv_s[1]), jnp.stack(fs_l, axis=0).reshape(n_ab, 1, H_A, bs).transpose(3, 0, 1, 2),
            st(cbp_l), st(cbs_l), st(ccp_l), st(ccs_l), st(hp_l), st(hs_l))
```
